```python
import jax, jax.numpy as jnp
from jax import lax
import numpy as np

D_MODEL = 1024
BATCH = 8
SEQ = 4096
DEPTH = 4

CHUNK = 64

D_RWKV = D_MODEL // 2
HEAD_SIZE = 64
N_HEADS = D_RWKV // HEAD_SIZE
D_DECAY_LORA = max(32, int(round(1.8 * D_MODEL ** 0.5 / 32)) * 32)
D_AAA_LORA = max(32, int(round(1.8 * D_MODEL ** 0.5 / 32)) * 32)
D_MV_LORA = max(32, int(round(0.7 * D_MODEL ** 0.5 / 32)) * 32)
D_GATE_LORA = max(32, int(round(0.6 * D_MODEL ** 0.8 / 32)) * 32)

D_POOL = D_MODEL // 2
POOL_WINDOWS = (2, 4, 8, 16)
N_POOL_GROUPS = len(POOL_WINDOWS)
POOL_GROUP = D_POOL // N_POOL_GROUPS

D_FF = ((8 * D_MODEL // 3 + 255) // 256) * 256

N_SHIFT = 3 * D_RWKV + D_DECAY_LORA + D_AAA_LORA + D_GATE_LORA
N_IN = N_SHIFT + D_POOL + 2 * D_MODEL
N_MOD = 6 * D_MODEL

EPS_RMS = 1e-6
EPS_GN = 64e-5

kernel_name = "hybrid_rwkv7_multipool_adaln_trunk"


def rms_norm(x, g):
    xf = x.astype(jnp.float32)
    y = xf * lax.rsqrt(jnp.mean(xf * xf, axis=-1, keepdims=True) + EPS_RMS)
    return (y * g.astype(jnp.float32)).astype(x.dtype)


def token_shift(z, mu):
    prev = jnp.pad(z, ((0, 0), (1, 0), (0, 0)))[:, :-1]
    return z + (prev - z) * mu


def wkv7_scan(r, w, k, v, kk, a):
    B, S, H, N = r.shape

    def to_chunks(t):
        return jnp.moveaxis(t, 1, 0).reshape(S // CHUNK, CHUNK, B, H, N)

    def step(state, inp):
        r_t, w_t, k_t, v_t, kk_t, a_t = inp
        sk = jnp.einsum('bhvk,bhk->bhv', state, kk_t)
        state = (state * w_t[:, :, None, :]
                 - sk[..., None] * (a_t * kk_t)[:, :, None, :]
                 + v_t[..., None] * k_t[:, :, None, :])
        return state, jnp.einsum('bhvk,bhk->bhv', state, r_t)

    def chunk_step(state, chunk_inp):
        return lax.scan(step, state, chunk_inp)

    state0 = jnp.zeros((B, H, N, N), jnp.float32)
    _, ys = lax.scan(chunk_step, state0, tuple(to_chunks(t) for t in (r, w, k, v, kk, a)))
    return jnp.moveaxis(ys.reshape(S, B, H, N), 0, 1)


def rwkv7_time_mix(ps, vl, v_first, w0, w_up, a0, a_up, v0, v_up, g_up, k_k, k_a, r_k, lnx_g, lnx_b):
    B, S, _ = ps.shape
    dt = ps.dtype
    f32 = jnp.float32
    o = 0
    r = ps[..., o:o + D_RWKV]; o += D_RWKV
    k = ps[..., o:o + D_RWKV]; o += D_RWKV
    v = ps[..., o:o + D_RWKV]; o += D_RWKV
    wl = ps[..., o:o + D_DECAY_LORA]; o += D_DECAY_LORA
    al = ps[..., o:o + D_AAA_LORA]; o += D_AAA_LORA
    gl = ps[..., o:o + D_GATE_LORA]

    log_w = -jax.nn.softplus(-(w0 + jnp.tanh(wl) @ w_up)) - 0.5
    decay = jnp.exp(-jnp.exp(log_w.astype(f32)))
    a = jax.nn.sigmoid(a0 + al @ a_up)
    g = jax.nn.sigmoid(gl) @ g_up
    if vl is None:
        v_first = v
    else:
        v = v + (v_first - v) * jax.nn.sigmoid(v0 + vl @ v_up)

    def heads(t):
        return t.reshape(B, S, N_HEADS, HEAD_SIZE).astype(f32)

    kk = heads(k * k_k)
    kk = kk / jnp.maximum(jnp.sqrt(jnp.sum(kk * kk, axis=-1, keepdims=True)), 1e-12)
    k = k * (1 + (a - 1) * k_a)
    r_h, k_h, v_h, a_h, w_h = heads(r), heads(k), heads(v), heads(a), heads(decay)

    y = wkv7_scan(r_h, w_h, k_h, v_h, kk, a_h)
    mean = jnp.mean(y, axis=-1, keepdims=True)
    var = jnp.mean(jnp.square(y - mean), axis=-1, keepdims=True)
    y = (y - mean) * lax.rsqrt(var + EPS_GN)
    y = y.reshape(B, S, D_RWKV) * lnx_g.astype(f32) + lnx_b.astype(f32)
    bonus = jnp.sum(r_h * k_h * r_k.astype(f32), axis=-1, keepdims=True) * v_h
    y = (y + bonus.reshape(B, S, D_RWKV)) * g.astype(f32)
    return y.astype(dt), v_first


def multiscale_pool(z, pool_w, pool_scale):
    B, S, _ = z.shape
    zg = z.reshape(B, S, N_POOL_GROUPS, POOL_GROUP).astype(jnp.float32)
    cs = jnp.cumsum(zg, axis=1)
    pos = jnp.arange(1, S + 1, dtype=jnp.float32)
    means = []
    for gi, win in enumerate(POOL_WINDOWS):
        cg = cs[:, :, gi]
        lower = jnp.pad(cg, ((0, 0), (win, 0), (0, 0)))[:, :S]
        count = jnp.minimum(pos, float(win))[None, :, None]
        means.append((cg - lower) / count)
    pooled = (jnp.stack(means, axis=2) - zg).astype(z.dtype)
    y = jnp.einsum('bsgc,gcd->bsgd', pooled, pool_w)
    return y.reshape(B, S, D_POOL) * pool_scale


def setup_inputs(seed: int = 0) -> dict:
    key = jax.random.key(seed)
    ks = iter(jax.random.split(key, 32))
    f32 = jnp.float32

    def nrm(shape, s):
        return jax.random.normal(next(ks), shape, f32) * s

    def uni(shape, lo, hi):
        return jax.random.uniform(next(ks), shape, f32, lo, hi)

    L, Lv = DEPTH, DEPTH - 1
    return {
        "x": nrm((BATCH, SEQ, D_MODEL), 1.0),
        "c": nrm((BATCH, D_MODEL), 1.0),
        "ada_w": nrm((L, D_MODEL, N_MOD), 0.5 * D_MODEL ** -0.5),
        "ada_b": nrm((L, N_MOD), 0.1),
        "norm1_g": 1.0 + nrm((L, D_MODEL), 0.1),
        "w_in": nrm((L, D_MODEL, N_IN), D_MODEL ** -0.5),
        "v_down": nrm((Lv, D_MODEL, D_MV_LORA), D_MODEL ** -0.5),
        "mu_shift": uni((L, N_SHIFT), 0.0, 1.0),
        "mu_v": uni((Lv, D_MV_LORA), 0.0, 1.0),
        "w0": uni((L, D_RWKV), -6.0, -1.0),
        "w_up": nrm((L, D_DECAY_LORA, D_RWKV), 0.5 * D_DECAY_LORA ** -0.5),
        "a0": nrm((L, D_RWKV), 0.1),
        "a_up": nrm((L, D_AAA_LORA, D_RWKV), 0.5 * D_AAA_LORA ** -0.5),
        "v0": nrm((Lv, D_RWKV), 0.1),
        "v_up": nrm((Lv, D_MV_LORA, D_RWKV), 0.5 * D_MV_LORA ** -0.5),
        "g_up": nrm((L, D_GATE_LORA, D_RWKV), D_GATE_LORA ** -0.5),
        "k_k": 0.85 + nrm((L, D_RWKV), 0.05),
        "k_a": 1.0 + nrm((L, D_RWKV), 0.05),
        "r_k": nrm((L, N_HEADS, HEAD_SIZE), 0.1),
        "lnx_g": 1.0 + nrm((L, D_RWKV), 0.1),
        "lnx_b": nrm((L, D_RWKV), 0.02),
        "pool_w": nrm((L, N_POOL_GROUPS, POOL_GROUP, POOL_GROUP), POOL_GROUP ** -0.5),
        "pool_scale": 1.0 + nrm((L, D_POOL), 0.1),
        "proj_a": nrm((L, D_RWKV, D_MODEL), D_RWKV ** -0.5),
        "proj_b": nrm((L, D_POOL, D_MODEL), D_POOL ** -0.5),
        "w_out": nrm((L, D_MODEL, D_MODEL), D_MODEL ** -0.5),
        "norm2_g": 1.0 + nrm((L, D_MODEL), 0.1),
        "w_gu": nrm((L, D_MODEL, 2 * D_FF), D_MODEL ** -0.5),
        "w_down": nrm((L, D_FF, D_MODEL), D_FF ** -0.5),
        "final_g": 1.0 + nrm((D_MODEL,), 0.1),
    }


def reference(x, c, ada_w, ada_b, norm1_g, w_in, v_down, mu_shift, mu_v, w0, w_up, a0, a_up, v0, v_up,
              g_up, k_k, k_a, r_k, lnx_g, lnx_b, pool_w, pool_scale, proj_a, proj_b, w_out, norm2_g,
              w_gu, w_down, final_g):
    B, S, D = x.shape
    c_act = jax.nn.silu(c)
    v_first = None
    for l in range(DEPTH):
        mod = c_act @ ada_w[l] + ada_b[l]
        sh1, sc1, gt1, sh2, sc2, gt2 = [m[:, None, :] for m in jnp.split(mod, 6, axis=-1)]

        h = rms_norm(x, norm1_g[l]) * (1 + sc1) + sh1
        if l == 0:
            p = h @ w_in[0]
            vl = None
        else:
            p = h @ jnp.concatenate([w_in[l], v_down[l - 1]], axis=1)
            vl = token_shift(p[..., N_IN:], mu_v[l - 1])
        ps = token_shift(p[..., :N_SHIFT], mu_shift[l])
        z_pool = p[..., N_SHIFT:N_SHIFT + D_POOL]
        br_gate = jax.nn.sigmoid(p[..., N_SHIFT + D_POOL:N_IN]).reshape(B, S, 2, D)

        y_a, v_first = rwkv7_time_mix(ps, vl, v_first, w0[l], w_up[l], a0[l], a_up[l],
                                      None if l == 0 else v0[l - 1], None if l == 0 else v_up[l - 1],
                                      g_up[l], k_k[l], k_a[l], r_k[l], lnx_g[l], lnx_b[l])
        y_b = multiscale_pool(z_pool, pool_w[l], pool_scale[l])
        mixed = br_gate[:, :, 0] * (y_a @ proj_a[l]) + br_gate[:, :, 1] * (y_b @ proj_b[l])
        x = x + gt1 * (mixed @ w_out[l])

        h2 = rms_norm(x, norm2_g[l]) * (1 + sc2) + sh2
        gate, up = jnp.split(h2 @ w_gu[l], 2, axis=-1)
        x = x + gt2 * ((jax.nn.silu(gate) * up) @ w_down[l])
    return rms_norm(x, final_g)
```

```python
import functools

import jax
import jax.numpy as jnp
from jax import lax
from jax.experimental import pallas as pl
from jax.experimental.pallas import tpu as pltpu

F32 = jnp.float32
BF16 = jnp.bfloat16

HEAD = 64
CHUNK = 64
QUAD = 4 * HEAD
LANES = 128
EPS_RMS = 1e-6
EPS_GN = 64e-5
POOL_WINDOWS = (2, 4, 8, 16)
POOL_HALO = 16
TOKEN_TILE = 256
FF_BLOCK = 512
VMEM_LIMIT = 56 * 1024 * 1024


def _dot(a, b):
    return jnp.dot(a, b, preferred_element_type=F32)


def _dot_nt(a, b):
    return lax.dot_general(a, b, (((1,), (1,)), ((), ())), preferred_element_type=F32)


def _dot_tn(a, b):
    return lax.dot_general(a, b, (((0,), (0,)), ((), ())), preferred_element_type=F32)


def _split(x):
    hi = x.astype(BF16)
    lo = (x - hi.astype(F32)).astype(BF16)
    return hi, lo


def _sigmoid(x):
    return 1.0 / (1.0 + jnp.exp(-x))


def _mod_kernel(c_ref, w_ref, b_ref, o_ref):
    c = c_ref[...]
    c_act = (c * _sigmoid(c)).astype(BF16)
    o_ref[0] = _dot(c_act, w_ref[0].astype(BF16)) + b_ref[0]


def _modulation(c, ada_w, ada_b):
    depth, d, n_mod = ada_w.shape
    batch = c.shape[0]
    nb = 1024
    return pl.pallas_call(
        _mod_kernel,
        out_shape=jax.ShapeDtypeStruct((depth, batch, n_mod), F32),
        grid=(depth, n_mod // nb),
        in_specs=[
            pl.BlockSpec((batch, d), lambda l, j: (0, 0)),
            pl.BlockSpec((1, d, nb), lambda l, j: (l, 0, j)),
            pl.BlockSpec((1, 1, nb), lambda l, j: (l, 0, j)),
        ],
        out_specs=pl.BlockSpec((1, batch, nb), lambda l, j: (l, 0, j)),
        name="adaln_mod",
    )(c, ada_w, ada_b.reshape(depth, 1, n_mod))


def _bd(xq, lane_lo):
    z = jnp.zeros((CHUNK, LANES), xq.dtype)
    zero = jnp.zeros((), xq.dtype)
    x0 = xq[:, :LANES]
    x1 = xq[:, LANES:]
    rows = [
        jnp.concatenate([jnp.where(lane_lo, x0, zero), z], axis=1),
        jnp.concatenate([jnp.where(lane_lo, zero, x0), z], axis=1),
        jnp.concatenate([z, jnp.where(lane_lo, x1, zero)], axis=1),
        jnp.concatenate([z, jnp.where(lane_lo, zero, x1)], axis=1),
    ]
    return jnp.concatenate(rows, axis=0)


def _mm3_bd(a, xq, lane_lo):
    ah, al = _split(a)
    xh, xl = _split(xq)
    bh = _bd(xh, lane_lo)
    bl = _bd(xl, lane_lo)
    return _dot(ah, bh) + (_dot(al, bh) + _dot(ah, bl))


def _mm3(a, b):
    ah, al = _split(a)
    bh, bl = _split(b)
    return _dot(ah, bh) + (_dot(al, bh) + _dot(ah, bl))


def _wkv_kernel(r_ref, lw_ref, k_ref, v_ref, kk_ref, b_ref, y_ref, h_ref):
    c = pl.program_id(1)

    @pl.when(c == 0)
    def _():
        h_ref[...] = jnp.zeros_like(h_ref)

    row = lax.broadcasted_iota(jnp.int32, (CHUNK, QUAD), 0)
    col = lax.broadcasted_iota(jnp.int32, (CHUNK, QUAD), 1) % HEAD
    strict = col < row
    incl = col <= row
    eye_q = (col == row).astype(F32)
    lane_lo = lax.broadcasted_iota(jnp.int32, (CHUNK, LANES), 1) < HEAD
    sq_r = lax.broadcasted_iota(jnp.int32, (QUAD, QUAD), 0)
    sq_c = lax.broadcasted_iota(jnp.int32, (QUAD, QUAD), 1)
    same_head = (sq_r // HEAD) == (sq_c // HEAD)
    on_diag = sq_r == sq_c

    tri = (lax.broadcasted_iota(jnp.int32, (CHUNK, CHUNK), 1)
           <= lax.broadcasted_iota(jnp.int32, (CHUNK, CHUNK), 0)).astype(BF16)
    lw = lw_ref[0]
    lw_hi = lw.astype(BF16)
    rem = lw - lw_hi.astype(F32)
    lw_mid = rem.astype(BF16)
    lw_lo = (rem - lw_mid.astype(F32)).astype(BF16)
    cum = _dot(tri, lw_hi) + (_dot(tri, lw_mid) + _dot(tri, lw_lo))
    cum_end = cum[CHUNK - 1:CHUNK, :]

    e_in = jnp.exp(cum)
    e_prev = jnp.exp(cum - lw)
    e_out = jnp.exp(-cum)
    e_tail = jnp.exp(cum_end - cum)
    p_end = jnp.exp(cum_end)

    r_t = r_ref[0] * e_in
    kk_t = kk_ref[0] * e_prev
    k_t = k_ref[0] * e_out
    b_t = b_ref[0] * e_out
    k_hat = k_ref[0] * e_tail
    b_hat = b_ref[0] * e_tail
    v_all = v_ref[0]

    n_quads = r_t.shape[1] // QUAD
    for q in range(n_quads):
        sl = slice(q * QUAD, (q + 1) * QUAD)
        rq, kkq, kq, bq, vq = r_t[:, sl], kk_t[:, sl], k_t[:, sl], b_t[:, sl], v_all[:, sl]

        lhs_h, lhs_l = _split(jnp.concatenate([kkq, rq], axis=0))
        kh, kl = _split(kq)
        bh, bl = _split(bq)
        kbd_h, kbd_l = _bd(kh, lane_lo), _bd(kl, lane_lo)
        bbd_h, bbd_l = _bd(bh, lane_lo), _bd(bl, lane_lo)
        a_k = _dot_nt(lhs_h, kbd_h) + (_dot_nt(lhs_l, kbd_h) + _dot_nt(lhs_h, kbd_l))
        a_b = _dot_nt(lhs_h, bbd_h) + (_dot_nt(lhs_l, bbd_h) + _dot_nt(lhs_h, bbd_l))
        a_kk = jnp.where(strict, a_k[:CHUNK], 0.0)
        a_rk = jnp.where(incl, a_k[CHUNK:], 0.0)
        a_kb = jnp.where(strict, a_b[:CHUNK], 0.0)
        a_rb = jnp.where(incl, a_b[CHUNK:], 0.0)

        t_inv = eye_q - a_kb
        pw = _mm3_bd(a_kb, a_kb, lane_lo)
        for _ in range(4):
            both = _mm3_bd(jnp.concatenate([pw, t_inv], axis=0), pw, lane_lo)
            pw = both[:CHUNK]
            t_inv = t_inv + both[CHUNK:]
        t_inv = t_inv + _mm3_bd(t_inv, pw, lane_lo)

        av = _mm3_bd(jnp.concatenate([a_kk, a_rk], axis=0), vq, lane_lo)
        akk_v, ark_v = av[:CHUNK], av[CHUNK:]
        kk_hat = _mm3_bd(t_inv, kkq, lane_lo)
        u0 = _mm3_bd(t_inv, akk_v, lane_lo)
        r_hat = rq - _mm3_bd(a_rb, kk_hat, lane_lo)
        y0 = ark_v - _mm3_bd(a_rb, u0, lane_lo)

        bhq, khq = b_hat[:, sl], k_hat[:, sl]
        m_full = _mm3(bhq.T, kk_hat)
        g_full = _mm3(khq.T, vq) - _mm3(bhq.T, u0)
        decay = jnp.where(on_diag, jnp.broadcast_to(p_end[:, sl], (QUAD, QUAD)), 0.0)
        m_bd = decay - jnp.where(same_head, m_full, 0.0)
        g_bd = jnp.where(same_head, g_full, 0.0)

        h0 = h_ref[q]
        out = _mm3(jnp.concatenate([r_hat, m_bd], axis=0), h0)
        y_ref[0, :, sl] = out[:CHUNK] + y0
        h_ref[q] = out[CHUNK:] + g_bd


def _wkv(r, lw, k, v, kk, b):
    batch, seq, width = r.shape
    spec = pl.BlockSpec((1, CHUNK, width), lambda i, c: (i, c, 0))
    return pl.pallas_call(
        _wkv_kernel,
        out_shape=jax.ShapeDtypeStruct((batch, seq, width), F32),
        grid=(batch, seq // CHUNK),
        in_specs=[spec] * 6,
        out_specs=spec,
        scratch_shapes=[pltpu.VMEM((width // QUAD, QUAD, QUAD), F32)],
        compiler_params=pltpu.CompilerParams(
            dimension_semantics=("arbitrary", "arbitrary"), vmem_limit_bytes=VMEM_LIMIT),
        name="wkv7_chunked",
    )(r, lw, k, v, kk, b)


C_R, C_K, C_V = 0, 512, 1024
C_WA = 1536
C_GV = 1664
N_SHIFTED = 1920
C_POOL = 1920
C_GATE = 2432
N_PROJ = 4480
D_RWKV = 512
N_GATE_LORA = 160
SHIFT_PAD = 8


def _rms_mod(x, g, scale, shift):
    ms = jnp.mean(x * x, axis=-1, keepdims=True)
    return x * lax.rsqrt(ms + EPS_RMS) * g * (1.0 + scale) + shift


def _inproj_kernel(has_vres, x_ref, mod_ref, g1_ref, w_ref, mu_ref, vec_ref, up1_ref, up2_ref, ones_ref,
                   *rest):
    if has_vres:
        vfirst_ref, rest = rest[0], rest[1:]
    (r_ref, lw_ref, k_ref, v_ref, kk_ref, b_ref, g_ref, bonus_ref, zp_ref, gate_ref, sh_ref) = rest
    s = pl.program_id(1)
    ts = x_ref.shape[1]
    d = x_ref.shape[2]

    mod = mod_ref[0]
    h = _rms_mod(x_ref[0], g1_ref[...], mod[:, d:2 * d], mod[:, 0:d]).astype(BF16)

    @pl.when(s == 0)
    def _():
        sh_ref[0:SHIFT_PAD, :] = jnp.zeros((SHIFT_PAD, N_SHIFTED), F32)

    @pl.when(s > 0)
    def _():
        sh_ref[0:SHIFT_PAD, :] = sh_ref[ts:ts + SHIFT_PAD, :]

    sh_ref[SHIFT_PAD:SHIFT_PAD + ts, :] = _dot(h, w_ref[:, 0:N_SHIFTED])
    zp_ref[0] = _dot(h, w_ref[:, C_POOL:C_GATE])
    for j in range(C_GATE, N_PROJ, 512):
        gate_ref[0, :, j - C_GATE:j - C_GATE + 512] = _sigmoid(_dot(h, w_ref[:, j:j + 512]))

    def shifted(c0, c1):
        cur = sh_ref[SHIFT_PAD:SHIFT_PAD + ts, c0:c1]
        prev = sh_ref[SHIFT_PAD - 1:SHIFT_PAD - 1 + ts, c0:c1]
        return cur + (prev - cur) * mu_ref[:, c0:c1]

    w0, a0, v0 = vec_ref[0:1, :], vec_ref[1:2, :], vec_ref[2:3, :]
    k_k, k_a, r_k = vec_ref[3:4, :], vec_ref[4:5, :], vec_ref[5:6, :]

    wa = shifted(C_WA, C_GV)
    lane = lax.broadcasted_iota(jnp.int32, wa.shape, 1)
    wa = jnp.where(lane < 64, jnp.tanh(wa), wa)
    lora1 = _dot(wa.astype(BF16), up1_ref[...])
    gv = shifted(C_GV, N_SHIFTED)
    lane = lax.broadcasted_iota(jnp.int32, gv.shape, 1)
    gv = jnp.where(lane < N_GATE_LORA, _sigmoid(gv), gv)
    lora2 = _dot(gv.astype(BF16), up2_ref[...])

    xw = w0 + lora1[:, :D_RWKV]
    softplus = jnp.maximum(-xw, 0.0) + jnp.log(1.0 + jnp.exp(-jnp.abs(xw)))
    lw_ref[0] = -jnp.exp(-softplus - 0.5)
    a = _sigmoid(a0 + lora1[:, D_RWKV:])
    g_ref[0] = lora2[:, :D_RWKV]

    r = shifted(C_R, C_K)
    k = shifted(C_K, C_V)
    v = shifted(C_V, C_WA)
    if has_vres:
        v = v + (vfirst_ref[0] - v) * _sigmoid(v0 + lora2[:, D_RWKV:])
    r_ref[0] = r
    v_ref[0] = v

    kk = k * k_k
    sq_hi, sq_lo = _split(kk * kk)
    n2 = _dot(sq_hi, ones_ref[...]) + _dot(sq_lo, ones_ref[...])
    kk = kk / jnp.maximum(jnp.sqrt(n2), 1e-12)
    kk_ref[0] = kk
    b_ref[0] = a * kk
    k = k * (1.0 + (a - 1.0) * k_a)
    k_ref[0] = k
    bonus_ref[0] = _dot((r * k * r_k).astype(BF16), ones_ref[...]) * v


def _inproj(x, mod, g1, w, mu, vecs, up1, up2, ones_bd, vfirst):
    batch, seq, d = x.shape
    ts = TOKEN_TILE
    has_vres = vfirst is not None
    tok = lambda n: pl.BlockSpec((1, ts, n), lambda i, s: (i, s, 0))
    full = lambda a: pl.BlockSpec(a.shape, lambda i, s: (0,) * a.ndim)
    in_specs = [tok(d), pl.BlockSpec((1, 1, mod.shape[2]), lambda i, s: (i, 0, 0)),
                full(g1), full(w), full(mu), full(vecs), full(up1), full(up2), full(ones_bd)]
    args = [x, mod, g1, w, mu, vecs, up1, up2, ones_bd]
    if has_vres:
        in_specs.append(tok(D_RWKV))
        args.append(vfirst)
    n_gate = N_PROJ - C_GATE
    out_shape = [jax.ShapeDtypeStruct((batch, seq, D_RWKV), F32)] * 9 + [
        jax.ShapeDtypeStruct((batch, seq, n_gate), F32)]
    out_specs = [tok(D_RWKV)] * 9 + [tok(n_gate)]
    return pl.pallas_call(
        functools.partial(_inproj_kernel, has_vres),
        out_shape=out_shape,
        grid=(batch, seq // ts),
        in_specs=in_specs,
        out_specs=out_specs,
        scratch_shapes=[pltpu.VMEM((ts + SHIFT_PAD, N_SHIFTED), F32)],
        compiler_params=pltpu.CompilerParams(
            dimension_semantics=("arbitrary", "arbitrary"), vmem_limit_bytes=VMEM_LIMIT),
        name="inproj_vres" if has_vres else "inproj",
    )(*args)


def _mix_kernel(y_ref, g_ref, bonus_ref, zp_ref, gate_ref, x_ref, mod_ref, vec_ref, avg_ref, poolw_ref,
                pa_ref, pb_ref, wout_ref, o_ref, halo_ref):
    s = pl.program_id(1)
    ts = x_ref.shape[1]
    d = x_ref.shape[2]
    lnx_g, lnx_b, pool_scale = vec_ref[0:1, :], vec_ref[1:2, :], vec_ref[2:3, :]

    y = y_ref[0]
    mean = _dot(y.astype(BF16), avg_ref[...])
    dev = y - mean
    var = _dot((dev * dev).astype(BF16), avg_ref[...])
    ya = (dev * lax.rsqrt(var + EPS_GN) * lnx_g + lnx_b + bonus_ref[0]) * g_ref[0]
    proj_a = _dot(ya.astype(BF16), pa_ref[...])

    @pl.when(s == 0)
    def _():
        halo_ref[0:POOL_HALO, :] = jnp.zeros((POOL_HALO, halo_ref.shape[1]), F32)

    @pl.when(s > 0)
    def _():
        halo_ref[0:POOL_HALO, :] = halo_ref[ts:ts + POOL_HALO, :]

    z = zp_ref[0]
    halo_ref[POOL_HALO:POOL_HALO + ts, :] = z
    pos = (s * ts + 1 + lax.broadcasted_iota(jnp.int32, (ts, LANES), 0)).astype(F32)
    pooled = []
    for gi, win in enumerate(POOL_WINDOWS):
        acc = halo_ref[:, gi * LANES:(gi + 1) * LANES]
        span = 1
        while span < win:
            acc = acc + pltpu.roll(acc, span, 0)
            span *= 2
        mean_g = acc[POOL_HALO:, :] / jnp.minimum(pos, float(win))
        pooled.append(mean_g - z[:, gi * LANES:(gi + 1) * LANES])
    pooled = jnp.concatenate(pooled, axis=1)
    yb = _dot(pooled.astype(BF16), poolw_ref[...]) * pool_scale
    proj_b = _dot(yb.astype(BF16), pb_ref[...])

    gate = gate_ref[0]
    mixed = gate[:, :d] * proj_a + gate[:, d:] * proj_b
    gt1 = mod_ref[0][:, 2 * d:3 * d]
    o_ref[0] = x_ref[0] + gt1 * _dot(mixed.astype(BF16), wout_ref[...])


def _mix(y, g, bonus, zp, gate, x, mod, vecs, avg_bd, pool_bd, proj_a, proj_b, w_out):
    batch, seq, d = x.shape
    ts = TOKEN_TILE
    tok = lambda n: pl.BlockSpec((1, ts, n), lambda i, s: (i, s, 0))
    full = lambda a: pl.BlockSpec(a.shape, lambda i, s: (0,) * a.ndim)
    return pl.pallas_call(
        _mix_kernel,
        out_shape=jax.ShapeDtypeStruct((batch, seq, d), F32),
        grid=(batch, seq // ts),
        in_specs=[tok(D_RWKV), tok(D_RWKV), tok(D_RWKV), tok(zp.shape[2]), tok(gate.shape[2]), tok(d),
                  pl.BlockSpec((1, 1, mod.shape[2]), lambda i, s: (i, 0, 0)),
                  full(vecs), full(avg_bd), full(pool_bd), full(proj_a), full(proj_b), full(w_out)],
        out_specs=tok(d),
        scratch_shapes=[pltpu.VMEM((ts + POOL_HALO, zp.shape[2]), F32)],
        compiler_params=pltpu.CompilerParams(
            dimension_semantics=("arbitrary", "arbitrary"), vmem_limit_bytes=VMEM_LIMIT),
        name="branch_mix",
    )(y, g, bonus, zp, gate, x, mod, vecs, avg_bd, pool_bd, proj_a, proj_b, w_out)


def _ffn_kernel(final, x_ref, mod_ref, g2_ref, wgu_ref, wd_ref, *rest):
    if final:
        gf_ref, o_ref, acc_ref = rest
    else:
        o_ref, acc_ref = rest
    d = x_ref.shape[2]
    d_ff = wd_ref.shape[0]
    mod = mod_ref[0]
    x = x_ref[0]
    h = _rms_mod(x, g2_ref[...], mod[:, 4 * d:5 * d], mod[:, 3 * d:4 * d]).astype(BF16)
    for j in range(0, d_ff, FF_BLOCK):
        w = min(FF_BLOCK, d_ff - j)
        gate = _dot(h, wgu_ref[:, j:j + w])
        up = _dot(h, wgu_ref[:, d_ff + j:d_ff + j + w])
        act = (gate * _sigmoid(gate) * up).astype(BF16)
        part = _dot(act, wd_ref[j:j + w, :])
        if j == 0:
            acc_ref[...] = part
        else:
            acc_ref[...] += part
    out = x + mod[:, 5 * d:6 * d] * acc_ref[...]
    if final:
        ms = jnp.mean(out * out, axis=-1, keepdims=True)
        out = out * lax.rsqrt(ms + EPS_RMS) * gf_ref[...]
    o_ref[0] = out


def _ffn(x, mod, g2, w_gu, w_down, final_g):
    batch, seq, d = x.shape
    ts = TOKEN_TILE
    final = final_g is not None
    tok = pl.BlockSpec((1, ts, d), lambda i, s: (i, s, 0))
    full = lambda a: pl.BlockSpec(a.shape, lambda i, s: (0,) * a.ndim)
    in_specs = [tok, pl.BlockSpec((1, 1, mod.shape[2]), lambda i, s: (i, 0, 0)), full(g2), full(w_gu), full(w_down)]
    args = [x, mod, g2, w_gu, w_down]
    if final:
        in_specs.append(full(final_g))
        args.append(final_g)
    return pl.pallas_call(
        functools.partial(_ffn_kernel, final),
        out_shape=jax.ShapeDtypeStruct((batch, seq, d), F32),
        grid=(batch, seq // ts),
        in_specs=in_specs,
        out_specs=tok,
        scratch_shapes=[pltpu.VMEM((ts, d), F32)],
        compiler_params=pltpu.CompilerParams(
            dimension_semantics=("arbitrary", "arbitrary"), vmem_limit_bytes=VMEM_LIMIT),
        name="swiglu_final" if final else "swiglu",
    )(*args)


def _block_diag(blocks):
    n = len(blocks)
    rows = []
    for i, blk in enumerate(blocks):
        rows.append(jnp.concatenate(
            [blk if j == i else jnp.zeros((blk.shape[0], blocks[j].shape[1]), blk.dtype) for j in range(n)], axis=1))
    return jnp.concatenate(rows, axis=0)


def _pack_inproj(w_in_l, v_down_l, mu_shift_l, mu_v_l):
    d = w_in_l.shape[0]
    n_shift = mu_shift_l.shape[0]
    n_vl = N_SHIFTED - n_shift - 64
    vd = jnp.zeros((d, n_vl), F32) if v_down_l is None else v_down_l
    mv = jnp.zeros((n_vl,), F32) if mu_v_l is None else mu_v_l
    w = jnp.concatenate([w_in_l[:, :n_shift], vd, jnp.zeros((d, 64), F32), w_in_l[:, n_shift:]], axis=1)
    mu = jnp.concatenate([mu_shift_l, mv, jnp.zeros((64,), F32)])
    return w.astype(BF16), mu.reshape(1, N_SHIFTED)


def kernel(x, c, ada_w, ada_b, norm1_g, w_in, v_down, mu_shift, mu_v, w0, w_up, a0, a_up, v0, v_up, g_up, k_k, k_a, r_k, lnx_g, lnx_b, pool_w, pool_scale, proj_a, proj_b, w_out, norm2_g, w_gu, w_down, final_g):
    depth = w_in.shape[0]
    d = x.shape[2]
    batch = x.shape[0]
    mods = _modulation(c, ada_w, ada_b).reshape(depth, batch, 1, 6 * d)
    n_heads = D_RWKV // HEAD
    ones_bd = _block_diag([jnp.ones((HEAD, HEAD), BF16)] * n_heads)
    avg_bd = _block_diag([jnp.full((HEAD, HEAD), 1.0 / HEAD, BF16)] * n_heads)
    zeros_r = jnp.zeros((D_RWKV,), F32)

    v_first = None
    for l in range(depth):
        w1, mu = _pack_inproj(w_in[l], v_down[l - 1] if l else None, mu_shift[l], mu_v[l - 1] if l else None)
        vecs = jnp.stack([w0[l], a0[l], v0[l - 1] if l else zeros_r, k_k[l], k_a[l], r_k[l].reshape(-1),
                          zeros_r, zeros_r])
        up1 = _block_diag([w_up[l], a_up[l]]).astype(BF16)
        n_vl = v_up.shape[1]
        up2 = jnp.concatenate([
            jnp.concatenate([g_up[l], jnp.zeros((N_GATE_LORA, D_RWKV), F32)], axis=1),
            jnp.concatenate([jnp.zeros((n_vl, D_RWKV), F32), v_up[l - 1] if l else jnp.zeros((n_vl, D_RWKV), F32)], axis=1),
            jnp.zeros((N_SHIFTED - C_GV - N_GATE_LORA - n_vl, 2 * D_RWKV), F32)], axis=0).astype(BF16)
        r, lw, k, v, kk, b, g, bonus, zp, gate = _inproj(
            x, mods[l], norm1_g[l].reshape(1, d), w1, mu, vecs, up1, up2, ones_bd, v_first)
        if l == 0:
            v_first = v
        y = _wkv(r, lw, k, v, kk, b)
        mix_vecs = jnp.stack([lnx_g[l], lnx_b[l], pool_scale[l]] + [zeros_r] * 5)
        pool_bd = _block_diag([pool_w[l, i] for i in range(pool_w.shape[1])]).astype(BF16)
        x = _mix(y, g, bonus, zp, gate, x, mods[l], mix_vecs, avg_bd, pool_bd,
                 proj_a[l].astype(BF16), proj_b[l].astype(BF16), w_out[l].astype(BF16))
        x = _ffn(x, mods[l], norm2_g[l].reshape(1, d), w_gu[l].astype(BF16), w_down[l].astype(BF16),
                 final_g.reshape(1, d) if l == depth - 1 else None)
    return x
```

```python
import functools

import jax
import jax.numpy as jnp
from jax import lax
from jax.experimental import pallas as pl
from jax.experimental.pallas import tpu as pltpu

F32 = jnp.float32
BF16 = jnp.bfloat16

HEAD = 64
CHUNK = 64
QUAD = 4 * HEAD
LANES = 128
EPS_RMS = 1e-6
EPS_GN = 64e-5
POOL_WINDOWS = (2, 4, 8, 16)
POOL_HALO = 16
TOKEN_TILE = 256
FF_BLOCK = 512
VMEM_LIMIT = 56 * 1024 * 1024


def _dot(a, b):
    return jnp.dot(a, b, preferred_element_type=F32)


def _dot_nt(a, b):
    return lax.dot_general(a, b, (((1,), (1,)), ((), ())), preferred_element_type=F32)


def _dot_tn(a, b):
    return lax.dot_general(a, b, (((0,), (0,)), ((), ())), preferred_element_type=F32)


def _split(x):
    hi = x.astype(BF16)
    lo = (x - hi.astype(F32)).astype(BF16)
    return hi, lo


def _sigmoid(x):
    return 1.0 / (1.0 + jnp.exp(-x))


def _mod_kernel(c_ref, w_ref, b_ref, o_ref):
    c = c_ref[...]
    c_act = (c * _sigmoid(c)).astype(BF16)
    o_ref[0] = _dot(c_act, w_ref[0].astype(BF16)) + b_ref[0]


def _modulation(c, ada_w, ada_b):
    depth, d, n_mod = ada_w.shape
    batch = c.shape[0]
    nb = 1024
    return pl.pallas_call(
        _mod_kernel,
        out_shape=jax.ShapeDtypeStruct((depth, batch, n_mod), F32),
        grid=(depth, n_mod // nb),
        in_specs=[
            pl.BlockSpec((batch, d), lambda l, j: (0, 0)),
            pl.BlockSpec((1, d, nb), lambda l, j: (l, 0, j)),
            pl.BlockSpec((1, 1, nb), lambda l, j: (l, 0, j)),
        ],
        out_specs=pl.BlockSpec((1, batch, nb), lambda l, j: (l, 0, j)),
        name="adaln_mod",
    )(c, ada_w, ada_b.reshape(depth, 1, n_mod))


def _bd(xq, lane_lo):
    z = jnp.zeros((CHUNK, LANES), xq.dtype)
    zero = jnp.zeros((), xq.dtype)
    x0 = xq[:, :LANES]
    x1 = xq[:, LANES:]
    rows = [
        jnp.concatenate([jnp.where(lane_lo, x0, zero), z], axis=1),
        jnp.concatenate([jnp.where(lane_lo, zero, x0), z], axis=1),
        jnp.concatenate([z, jnp.where(lane_lo, x1, zero)], axis=1),
        jnp.concatenate([z, jnp.where(lane_lo, zero, x1)], axis=1),
    ]
    return jnp.concatenate(rows, axis=0)


def _mm_bd(a, xq, lane_lo):
    return _dot(a.astype(BF16), _bd(xq.astype(BF16), lane_lo))


def _mm3_bd(a, xq, lane_lo):
    m = a.shape[0]
    ah, al = _split(a)
    xh, xl = _split(xq)
    both = _dot(jnp.concatenate([ah, al], axis=0), _bd(xh, lane_lo))
    return both[:m] + (both[m:] + _dot(ah, _bd(xl, lane_lo)))


def _head_blocks(full, head_of_lane):
    out = jnp.zeros((HEAD, QUAD), full.dtype)
    for h in range(QUAD // HEAD):
        out = jnp.where(head_of_lane == h, full[h * HEAD:(h + 1) * HEAD, :], out)
    return out


WKV_CHUNKS_PER_STEP = 4


def _wkv_kernel(r_ref, lw_ref, k_ref, v_ref, kk_ref, b_ref, y_ref, h_ref):
    @pl.when(pl.program_id(1) == 0)
    def _():
        h_ref[...] = jnp.zeros_like(h_ref)

    row = lax.broadcasted_iota(jnp.int32, (CHUNK, QUAD), 0)
    lane = lax.broadcasted_iota(jnp.int32, (CHUNK, QUAD), 1)
    col = lane % HEAD
    head_of_lane = lane // HEAD
    strict = col < row
    incl = col <= row
    eye_q = (col == row).astype(F32)
    lane_lo = lax.broadcasted_iota(jnp.int32, (CHUNK, LANES), 1) < HEAD
    tri = (lax.broadcasted_iota(jnp.int32, (CHUNK, CHUNK), 1)
           <= lax.broadcasted_iota(jnp.int32, (CHUNK, CHUNK), 0)).astype(BF16)

    n_chunks = r_ref.shape[1] // CHUNK
    n_quads = r_ref.shape[2] // QUAD
    chains = []
    for ci in range(n_chunks):
        rows = slice(ci * CHUNK, (ci + 1) * CHUNK)
        lw = lw_ref[0, rows, :]
        lw_hi = lw.astype(BF16)
        rem = lw - lw_hi.astype(F32)
        lw_mid = rem.astype(BF16)
        lw_lo = (rem - lw_mid.astype(F32)).astype(BF16)
        cum = _dot(tri, lw_hi) + (_dot(tri, lw_mid) + _dot(tri, lw_lo))
        cum_end = cum[CHUNK - 1:CHUNK, :]

        e_out = jnp.exp(-cum)
        e_tail = jnp.exp(cum_end - cum)
        p_end = jnp.exp(cum_end)
        r_t = r_ref[0, rows, :] * jnp.exp(cum)
        kk_t = kk_ref[0, rows, :] * jnp.exp(cum - lw)
        k_raw = k_ref[0, rows, :]
        b_raw = b_ref[0, rows, :]
        k_t, b_t = k_raw * e_out, b_raw * e_out
        k_hat, b_hat = k_raw * e_tail, b_raw * e_tail
        v_all = v_ref[0, rows, :]
        for q in range(n_quads):
            sl = slice(q * QUAD, (q + 1) * QUAD)
            chains.append(dict(ci=ci, q=q, rq=r_t[:, sl], kkq=kk_t[:, sl], vq=v_all[:, sl], kq=k_t[:, sl],
                               bq=b_t[:, sl], khq=k_hat[:, sl], bhq=b_hat[:, sl], p_end=p_end[:, sl]))

    for ch in chains:
        lhs = jnp.concatenate([ch["kkq"], ch["rq"]], axis=0).astype(BF16)
        a_k = _dot_nt(lhs, _bd(ch["kq"].astype(BF16), lane_lo))
        a_b = _dot_nt(lhs, _bd(ch["bq"].astype(BF16), lane_lo))
        ch["a_kk"] = jnp.where(strict, a_k[:CHUNK], 0.0)
        ch["a_rk"] = jnp.where(incl, a_k[CHUNK:], 0.0)
        ch["a_kb"] = jnp.where(strict, a_b[:CHUNK], 0.0)
        ch["a_rb"] = jnp.where(incl, a_b[CHUNK:], 0.0)

    for ch in chains:
        ch["t"] = eye_q - ch["a_kb"]
        ch["pw"] = _mm_bd(ch["a_kb"], ch["a_kb"], lane_lo)
        ch["av"] = _mm_bd(jnp.concatenate([ch["a_kk"], ch["a_rk"]], axis=0), ch["vq"], lane_lo)
    for _ in range(4):
        for ch in chains:
            both = _mm_bd(jnp.concatenate([ch["pw"], ch["t"]], axis=0), ch["pw"], lane_lo)
            ch["pw"] = both[:CHUNK]
            ch["t"] = ch["t"] + both[CHUNK:]
    for ch in chains:
        ch["t"] = ch["t"] + _mm_bd(ch["t"], ch["pw"], lane_lo)

    for ch in chains:
        ch["kk_hat"] = _mm_bd(ch["t"], ch["kkq"], lane_lo)
        ch["u0"] = _mm_bd(ch["t"], ch["av"][:CHUNK], lane_lo)
    for ch in chains:
        ch["r_hat"] = ch["rq"] - _mm_bd(ch["a_rb"], ch["kk_hat"], lane_lo)
        ch["y0"] = ch["av"][CHUNK:] - _mm_bd(ch["a_rb"], ch["u0"], lane_lo)
        bt = ch["bhq"].T.astype(BF16)
        kt = ch["khq"].T.astype(BF16)
        m_full = _dot(bt, ch["kk_hat"].astype(BF16))
        g_full = _dot(kt, ch["vq"].astype(BF16)) - _dot(bt, ch["u0"].astype(BF16))
        ch["m_q"] = eye_q * ch["p_end"] - _head_blocks(m_full, head_of_lane)
        ch["g_q"] = _head_blocks(g_full, head_of_lane)

    state = [h_ref[q] for q in range(n_quads)]
    for ch in chains:
        ci, q = ch["ci"], ch["q"]
        out = _mm3_bd(jnp.concatenate([ch["r_hat"], ch["m_q"]], axis=0), state[q], lane_lo)
        y_ref[0, ci * CHUNK:(ci + 1) * CHUNK, q * QUAD:(q + 1) * QUAD] = out[:CHUNK] + ch["y0"]
        state[q] = out[CHUNK:] + ch["g_q"]
    for q in range(n_quads):
        h_ref[q] = state[q]


def _wkv(r, lw, k, v, kk, b):
    batch, seq, width = r.shape
    rows = CHUNK * WKV_CHUNKS_PER_STEP
    spec = pl.BlockSpec((1, rows, width), lambda i, c: (i, c, 0))
    return pl.pallas_call(
        _wkv_kernel,
        out_shape=jax.ShapeDtypeStruct((batch, seq, width), F32),
        grid=(batch, seq // rows),
        in_specs=[spec] * 6,
        out_specs=spec,
        scratch_shapes=[pltpu.VMEM((width // QUAD, HEAD, QUAD), F32)],
        compiler_params=pltpu.CompilerParams(
            dimension_semantics=("arbitrary", "arbitrary"), vmem_limit_bytes=VMEM_LIMIT),
        name="wkv7_chunked",
    )(r, lw, k, v, kk, b)


C_R, C_K, C_V = 0, 512, 1024
C_WA = 1536
C_GV = 1664
N_SHIFTED = 1920
C_POOL = 1920
C_GATE = 2432
N_PROJ = 4480
D_RWKV = 512
N_GATE_LORA = 160
SHIFT_PAD = 8


def _rms_mod(x, g, scale, shift):
    ms = jnp.mean(x * x, axis=-1, keepdims=True)
    return x * lax.rsqrt(ms + EPS_RMS) * g * (1.0 + scale) + shift


def _inproj_kernel(has_vres, x_ref, mod_ref, g1_ref, w_ref, mu_ref, vec_ref, up1_ref, up2_ref, ones_ref,
                   *rest):
    if has_vres:
        vfirst_ref, rest = rest[0], rest[1:]
    (r_ref, lw_ref, k_ref, v_ref, kk_ref, b_ref, g_ref, bonus_ref, zp_ref, gate_ref, sh_ref) = rest
    s = pl.program_id(1)
    ts = x_ref.shape[1]
    d = x_ref.shape[2]

    mod = mod_ref[0]
    h = _rms_mod(x_ref[0], g1_ref[...], mod[:, d:2 * d], mod[:, 0:d]).astype(BF16)

    @pl.when(s == 0)
    def _():
        sh_ref[0:SHIFT_PAD, :] = jnp.zeros((SHIFT_PAD, N_SHIFTED), F32)

    @pl.when(s > 0)
    def _():
        sh_ref[0:SHIFT_PAD, :] = sh_ref[ts:ts + SHIFT_PAD, :]

    sh_ref[SHIFT_PAD:SHIFT_PAD + ts, :] = _dot(h, w_ref[:, 0:N_SHIFTED])
    zp_ref[0] = _dot(h, w_ref[:, C_POOL:C_GATE])
    for j in range(C_GATE, N_PROJ, 512):
        gate_ref[0, :, j - C_GATE:j - C_GATE + 512] = _sigmoid(_dot(h, w_ref[:, j:j + 512]))

    def shifted(c0, c1):
        cur = sh_ref[SHIFT_PAD:SHIFT_PAD + ts, c0:c1]
        prev = sh_ref[SHIFT_PAD - 1:SHIFT_PAD - 1 + ts, c0:c1]
        return cur + (prev - cur) * mu_ref[:, c0:c1]

    w0, a0, v0 = vec_ref[0:1, :], vec_ref[1:2, :], vec_ref[2:3, :]
    k_k, k_a, r_k = vec_ref[3:4, :], vec_ref[4:5, :], vec_ref[5:6, :]

    wa = shifted(C_WA, C_GV)
    lane = lax.broadcasted_iota(jnp.int32, wa.shape, 1)
    wa = jnp.where(lane < 64, jnp.tanh(wa), wa)
    lora1 = _dot(wa.astype(BF16), up1_ref[...])
    gv = shifted(C_GV, N_SHIFTED)
    lane = lax.broadcasted_iota(jnp.int32, gv.shape, 1)
    gv = jnp.where(lane < N_GATE_LORA, _sigmoid(gv), gv)
    lora2 = _dot(gv.astype(BF16), up2_ref[...])

    xw = w0 + lora1[:, :D_RWKV]
    softplus = jnp.maximum(-xw, 0.0) + jnp.log(1.0 + jnp.exp(-jnp.abs(xw)))
    lw_ref[0] = -jnp.exp(-softplus - 0.5)
    a = _sigmoid(a0 + lora1[:, D_RWKV:])
    g_ref[0] = lora2[:, :D_RWKV]

    r = shifted(C_R, C_K)
    k = shifted(C_K, C_V)
    v = shifted(C_V, C_WA)
    if has_vres:
        v = v + (vfirst_ref[0] - v) * _sigmoid(v0 + lora2[:, D_RWKV:])
    r_ref[0] = r
    v_ref[0] = v

    kk = k * k_k
    sq_hi, sq_lo = _split(kk * kk)
    n2 = _dot(sq_hi, ones_ref[...]) + _dot(sq_lo, ones_ref[...])
    kk = kk / jnp.maximum(jnp.sqrt(n2), 1e-12)
    kk_ref[0] = kk
    b_ref[0] = a * kk
    k = k * (1.0 + (a - 1.0) * k_a)
    k_ref[0] = k
    bonus_ref[0] = _dot((r * k * r_k).astype(BF16), ones_ref[...]) * v


def _inproj(x, mod, g1, w, mu, vecs, up1, up2, ones_bd, vfirst):
    batch, seq, d = x.shape
    ts = TOKEN_TILE
    has_vres = vfirst is not None
    tok = lambda n: pl.BlockSpec((1, ts, n), lambda i, s: (i, s, 0))
    full = lambda a: pl.BlockSpec(a.shape, lambda i, s: (0,) * a.ndim)
    in_specs = [tok(d), pl.BlockSpec((1, 1, mod.shape[2]), lambda i, s: (i, 0, 0)),
                full(g1), full(w), full(mu), full(vecs), full(up1), full(up2), full(ones_bd)]
    args = [x, mod, g1, w, mu, vecs, up1, up2, ones_bd]
    if has_vres:
        in_specs.append(tok(D_RWKV))
        args.append(vfirst)
    n_gate = N_PROJ - C_GATE
    out_shape = [jax.ShapeDtypeStruct((batch, seq, D_RWKV), F32)] * 9 + [
        jax.ShapeDtypeStruct((batch, seq, n_gate), F32)]
    out_specs = [tok(D_RWKV)] * 9 + [tok(n_gate)]
    return pl.pallas_call(
        functools.partial(_inproj_kernel, has_vres),
        out_shape=out_shape,
        grid=(batch, seq // ts),
        in_specs=in_specs,
        out_specs=out_specs,
        scratch_shapes=[pltpu.VMEM((ts + SHIFT_PAD, N_SHIFTED), F32)],
        compiler_params=pltpu.CompilerParams(
            dimension_semantics=("arbitrary", "arbitrary"), vmem_limit_bytes=VMEM_LIMIT),
        name="inproj_vres" if has_vres else "inproj",
    )(*args)


def _mix_kernel(y_ref, g_ref, bonus_ref, zp_ref, gate_ref, x_ref, mod_ref, vec_ref, avg_ref, poolw_ref,
                pa_ref, pb_ref, wout_ref, o_ref, halo_ref):
    s = pl.program_id(1)
    ts = x_ref.shape[1]
    d = x_ref.shape[2]
    lnx_g, lnx_b, pool_scale = vec_ref[0:1, :], vec_ref[1:2, :], vec_ref[2:3, :]

    y = y_ref[0]
    mean = _dot(y.astype(BF16), avg_ref[...])
    dev = y - mean
    var = _dot((dev * dev).astype(BF16), avg_ref[...])
    ya = (dev * lax.rsqrt(var + EPS_GN) * lnx_g + lnx_b + bonus_ref[0]) * g_ref[0]
    proj_a = _dot(ya.astype(BF16), pa_ref[...])

    @pl.when(s == 0)
    def _():
        halo_ref[0:POOL_HALO, :] = jnp.zeros((POOL_HALO, halo_ref.shape[1]), F32)

    @pl.when(s > 0)
    def _():
        halo_ref[0:POOL_HALO, :] = halo_ref[ts:ts + POOL_HALO, :]

    z = zp_ref[0]
    halo_ref[POOL_HALO:POOL_HALO + ts, :] = z
    pos = (s * ts + 1 + lax.broadcasted_iota(jnp.int32, (ts, LANES), 0)).astype(F32)
    pooled = []
    for gi, win in enumerate(POOL_WINDOWS):
        acc = halo_ref[:, gi * LANES:(gi + 1) * LANES]
        span = 1
        while span < win:
            acc = acc + pltpu.roll(acc, span, 0)
            span *= 2
        mean_g = acc[POOL_HALO:, :] / jnp.minimum(pos, float(win))
        pooled.append(mean_g - z[:, gi * LANES:(gi + 1) * LANES])
    pooled = jnp.concatenate(pooled, axis=1)
    yb = _dot(pooled.astype(BF16), poolw_ref[...]) * pool_scale
    proj_b = _dot(yb.astype(BF16), pb_ref[...])

    gate = gate_ref[0]
    mixed = gate[:, :d] * proj_a + gate[:, d:] * proj_b
    gt1 = mod_ref[0][:, 2 * d:3 * d]
    o_ref[0] = x_ref[0] + gt1 * _dot(mixed.astype(BF16), wout_ref[...])


def _mix(y, g, bonus, zp, gate, x, mod, vecs, avg_bd, pool_bd, proj_a, proj_b, w_out):
    batch, seq, d = x.shape
    ts = TOKEN_TILE
    tok = lambda n: pl.BlockSpec((1, ts, n), lambda i, s: (i, s, 0))
    full = lambda a: pl.BlockSpec(a.shape, lambda i, s: (0,) * a.ndim)
    return pl.pallas_call(
        _mix_kernel,
        out_shape=jax.ShapeDtypeStruct((batch, seq, d), F32),
        grid=(batch, seq // ts),
        in_specs=[tok(D_RWKV), tok(D_RWKV), tok(D_RWKV), tok(zp.shape[2]), tok(gate.shape[2]), tok(d),
                  pl.BlockSpec((1, 1, mod.shape[2]), lambda i, s: (i, 0, 0)),
                  full(vecs), full(avg_bd), full(pool_bd), full(proj_a), full(proj_b), full(w_out)],
        out_specs=tok(d),
        scratch_shapes=[pltpu.VMEM((ts + POOL_HALO, zp.shape[2]), F32)],
        compiler_params=pltpu.CompilerParams(
            dimension_semantics=("arbitrary", "arbitrary"), vmem_limit_bytes=VMEM_LIMIT),
        name="branch_mix",
    )(y, g, bonus, zp, gate, x, mod, vecs, avg_bd, pool_bd, proj_a, proj_b, w_out)


def _ffn_kernel(final, x_ref, mod_ref, g2_ref, wgu_ref, wd_ref, *rest):
    if final:
        gf_ref, o_ref, acc_ref = rest
    else:
        o_ref, acc_ref = rest
    d = x_ref.shape[2]
    d_ff = wd_ref.shape[0]
    mod = mod_ref[0]
    x = x_ref[0]
    h = _rms_mod(x, g2_ref[...], mod[:, 4 * d:5 * d], mod[:, 3 * d:4 * d]).astype(BF16)
    for j in range(0, d_ff, FF_BLOCK):
        w = min(FF_BLOCK, d_ff - j)
        gate = _dot(h, wgu_ref[:, j:j + w])
        up = _dot(h, wgu_ref[:, d_ff + j:d_ff + j + w])
        act = (gate * _sigmoid(gate) * up).astype(BF16)
        part = _dot(act, wd_ref[j:j + w, :])
        if j == 0:
            acc_ref[...] = part
        else:
            acc_ref[...] += part
    out = x + mod[:, 5 * d:6 * d] * acc_ref[...]
    if final:
        ms = jnp.mean(out * out, axis=-1, keepdims=True)
        out = out * lax.rsqrt(ms + EPS_RMS) * gf_ref[...]
    o_ref[0] = out


def _ffn(x, mod, g2, w_gu, w_down, final_g):
    batch, seq, d = x.shape
    ts = TOKEN_TILE
    final = final_g is not None
    tok = pl.BlockSpec((1, ts, d), lambda i, s: (i, s, 0))
    full = lambda a: pl.BlockSpec(a.shape, lambda i, s: (0,) * a.ndim)
    in_specs = [tok, pl.BlockSpec((1, 1, mod.shape[2]), lambda i, s: (i, 0, 0)), full(g2), full(w_gu), full(w_down)]
    args = [x, mod, g2, w_gu, w_down]
    if final:
        in_specs.append(full(final_g))
        args.append(final_g)
    return pl.pallas_call(
        functools.partial(_ffn_kernel, final),
        out_shape=jax.ShapeDtypeStruct((batch, seq, d), F32),
        grid=(batch, seq // ts),
        in_specs=in_specs,
        out_specs=tok,
        scratch_shapes=[pltpu.VMEM((ts, d), F32)],
        compiler_params=pltpu.CompilerParams(
            dimension_semantics=("arbitrary", "arbitrary"), vmem_limit_bytes=VMEM_LIMIT),
        name="swiglu_final" if final else "swiglu",
    )(*args)


def _block_diag(blocks):
    n = len(blocks)
    rows = []
    for i, blk in enumerate(blocks):
        rows.append(jnp.concatenate(
            [blk if j == i else jnp.zeros((blk.shape[0], blocks[j].shape[1]), blk.dtype) for j in range(n)], axis=1))
    return jnp.concatenate(rows, axis=0)


def _pack_inproj(w_in_l, v_down_l, mu_shift_l, mu_v_l):
    d = w_in_l.shape[0]
    n_shift = mu_shift_l.shape[0]
    n_vl = N_SHIFTED - n_shift - 64
    vd = jnp.zeros((d, n_vl), F32) if v_down_l is None else v_down_l
    mv = jnp.zeros((n_vl,), F32) if mu_v_l is None else mu_v_l
    w = jnp.concatenate([w_in_l[:, :n_shift], vd, jnp.zeros((d, 64), F32), w_in_l[:, n_shift:]], axis=1)
    mu = jnp.concatenate([mu_shift_l, mv, jnp.zeros((64,), F32)])
    return w.astype(BF16), mu.reshape(1, N_SHIFTED)


def kernel(x, c, ada_w, ada_b, norm1_g, w_in, v_down, mu_shift, mu_v, w0, w_up, a0, a_up, v0, v_up, g_up, k_k, k_a, r_k, lnx_g, lnx_b, pool_w, pool_scale, proj_a, proj_b, w_out, norm2_g, w_gu, w_down, final_g):
    depth = w_in.shape[0]
    d = x.shape[2]
    batch = x.shape[0]
    mods = _modulation(c, ada_w, ada_b).reshape(depth, batch, 1, 6 * d)
    n_heads = D_RWKV // HEAD
    ones_bd = _block_diag([jnp.ones((HEAD, HEAD), BF16)] * n_heads)
    avg_bd = _block_diag([jnp.full((HEAD, HEAD), 1.0 / HEAD, BF16)] * n_heads)
    zeros_r = jnp.zeros((D_RWKV,), F32)

    v_first = None
    for l in range(depth):
        w1, mu = _pack_inproj(w_in[l], v_down[l - 1] if l else None, mu_shift[l], mu_v[l - 1] if l else None)
        vecs = jnp.stack([w0[l], a0[l], v0[l - 1] if l else zeros_r, k_k[l], k_a[l], r_k[l].reshape(-1),
                          zeros_r, zeros_r])
        up1 = _block_diag([w_up[l], a_up[l]]).astype(BF16)
        n_vl = v_up.shape[1]
        up2 = jnp.concatenate([
            jnp.concatenate([g_up[l], jnp.zeros((N_GATE_LORA, D_RWKV), F32)], axis=1),
            jnp.concatenate([jnp.zeros((n_vl, D_RWKV), F32), v_up[l - 1] if l else jnp.zeros((n_vl, D_RWKV), F32)], axis=1),
            jnp.zeros((N_SHIFTED - C_GV - N_GATE_LORA - n_vl, 2 * D_RWKV), F32)], axis=0).astype(BF16)
        r, lw, k, v, kk, b, g, bonus, zp, gate = _inproj(
            x, mods[l], norm1_g[l].reshape(1, d), w1, mu, vecs, up1, up2, ones_bd, v_first)
        if l == 0:
            v_first = v
        y = _wkv(r, lw, k, v, kk, b)
        mix_vecs = jnp.stack([lnx_g[l], lnx_b[l], pool_scale[l]] + [zeros_r] * 5)
        pool_bd = _block_diag([pool_w[l, i] for i in range(pool_w.shape[1])]).astype(BF16)
        x = _mix(y, g, bonus, zp, gate, x, mods[l], mix_vecs, avg_bd, pool_bd,
                 proj_a[l].astype(BF16), proj_b[l].astype(BF16), w_out[l].astype(BF16))
        x = _ffn(x, mods[l], norm2_g[l].reshape(1, d), w_gu[l].astype(BF16), w_down[l].astype(BF16),
                 final_g.reshape(1, d) if l == depth - 1 else None)
    return x
```

```python
import functools

import jax
import jax.numpy as jnp
from jax import lax
from jax.experimental import pallas as pl
from jax.experimental.pallas import tpu as pltpu

F32 = jnp.float32
BF16 = jnp.bfloat16

HEAD = 64
CHUNK = 64
QUAD = 4 * HEAD
LANES = 128
EPS_RMS = 1e-6
EPS_GN = 64e-5
POOL_WINDOWS = (2, 4, 8, 16)
POOL_HALO = 16
TOKEN_TILE = 256
FF_BLOCK = 512
VMEM_LIMIT = 56 * 1024 * 1024


def _dot(a, b):
    return jnp.dot(a, b, preferred_element_type=F32)


def _dot_nt(a, b):
    return lax.dot_general(a, b, (((1,), (1,)), ((), ())), preferred_element_type=F32)


def _dot_tn(a, b):
    return lax.dot_general(a, b, (((0,), (0,)), ((), ())), preferred_element_type=F32)


def _split(x):
    hi = x.astype(BF16)
    lo = (x - hi.astype(F32)).astype(BF16)
    return hi, lo


def _sigmoid(x):
    return 1.0 / (1.0 + jnp.exp(-x))


def _mod_kernel(c_ref, w_ref, b_ref, o_ref):
    c = c_ref[...]
    c_act = (c * _sigmoid(c)).astype(BF16)
    o_ref[0] = _dot(c_act, w_ref[0].astype(BF16)) + b_ref[0]


def _modulation(c, ada_w, ada_b):
    depth, d, n_mod = ada_w.shape
    batch = c.shape[0]
    nb = 1024
    return pl.pallas_call(
        _mod_kernel,
        out_shape=jax.ShapeDtypeStruct((depth, batch, n_mod), F32),
        grid=(depth, n_mod // nb),
        in_specs=[
            pl.BlockSpec((batch, d), lambda l, j: (0, 0)),
            pl.BlockSpec((1, d, nb), lambda l, j: (l, 0, j)),
            pl.BlockSpec((1, 1, nb), lambda l, j: (l, 0, j)),
        ],
        out_specs=pl.BlockSpec((1, batch, nb), lambda l, j: (l, 0, j)),
        name="adaln_mod",
    )(c, ada_w, ada_b.reshape(depth, 1, n_mod))


def _bd(xq, lane_lo):
    z = jnp.zeros((CHUNK, LANES), xq.dtype)
    zero = jnp.zeros((), xq.dtype)
    x0 = xq[:, :LANES]
    x1 = xq[:, LANES:]
    rows = [
        jnp.concatenate([jnp.where(lane_lo, x0, zero), z], axis=1),
        jnp.concatenate([jnp.where(lane_lo, zero, x0), z], axis=1),
        jnp.concatenate([z, jnp.where(lane_lo, x1, zero)], axis=1),
        jnp.concatenate([z, jnp.where(lane_lo, zero, x1)], axis=1),
    ]
    return jnp.concatenate(rows, axis=0)


def _mm_bd(a, xq, lane_lo):
    return _dot(a.astype(BF16), _bd(xq.astype(BF16), lane_lo))


def _mm3_bd(a, xq, lane_lo):
    m = a.shape[0]
    ah, al = _split(a)
    xh, xl = _split(xq)
    both = _dot(jnp.concatenate([ah, al], axis=0), _bd(xh, lane_lo))
    return both[:m] + (both[m:] + _dot(ah, _bd(xl, lane_lo)))


def _head_blocks(full, head_of_lane):
    out = jnp.zeros((HEAD, QUAD), full.dtype)
    for h in range(QUAD // HEAD):
        out = jnp.where(head_of_lane == h, full[h * HEAD:(h + 1) * HEAD, :], out)
    return out


WKV_CHUNKS_PER_STEP = 4


def _wkv_kernel(r_ref, lw_ref, k_ref, v_ref, kk_ref, b_ref, y_ref, h_ref):
    @pl.when(pl.program_id(1) == 0)
    def _():
        h_ref[...] = jnp.zeros_like(h_ref)

    row = lax.broadcasted_iota(jnp.int32, (CHUNK, QUAD), 0)
    lane = lax.broadcasted_iota(jnp.int32, (CHUNK, QUAD), 1)
    col = lane % HEAD
    head_of_lane = lane // HEAD
    strict = col < row
    incl = col <= row
    eye_q = (col == row).astype(F32)
    lane_lo = lax.broadcasted_iota(jnp.int32, (CHUNK, LANES), 1) < HEAD
    tri = (lax.broadcasted_iota(jnp.int32, (CHUNK, CHUNK), 1)
           <= lax.broadcasted_iota(jnp.int32, (CHUNK, CHUNK), 0)).astype(BF16)

    n_chunks = r_ref.shape[1] // CHUNK
    n_quads = r_ref.shape[2] // QUAD
    chains = []
    for ci in range(n_chunks):
        rows = slice(ci * CHUNK, (ci + 1) * CHUNK)
        lw = lw_ref[0, rows, :]
        lw_hi = lw.astype(BF16)
        rem = lw - lw_hi.astype(F32)
        lw_mid = rem.astype(BF16)
        lw_lo = (rem - lw_mid.astype(F32)).astype(BF16)
        cum = _dot(tri, lw_hi) + (_dot(tri, lw_mid) + _dot(tri, lw_lo))
        cum_end = cum[CHUNK - 1:CHUNK, :]

        e_out = jnp.exp(-cum)
        e_tail = jnp.exp(cum_end - cum)
        p_end = jnp.exp(cum_end)
        r_t = r_ref[0, rows, :] * jnp.exp(cum)
        kk_t = kk_ref[0, rows, :] * jnp.exp(cum - lw)
        k_raw = k_ref[0, rows, :]
        b_raw = b_ref[0, rows, :]
        k_t, b_t = k_raw * e_out, b_raw * e_out
        k_hat, b_hat = k_raw * e_tail, b_raw * e_tail
        v_all = v_ref[0, rows, :]
        for q in range(n_quads):
            sl = slice(q * QUAD, (q + 1) * QUAD)
            chains.append(dict(ci=ci, q=q, rq=r_t[:, sl], kkq=kk_t[:, sl], vq=v_all[:, sl], kq=k_t[:, sl],
                               bq=b_t[:, sl], khq=k_hat[:, sl], bhq=b_hat[:, sl], p_end=p_end[:, sl]))

    for ch in chains:
        lhs = jnp.concatenate([ch["kkq"], ch["rq"]], axis=0).astype(BF16)
        a_k = _dot_nt(lhs, _bd(ch["kq"].astype(BF16), lane_lo))
        a_b = _dot_nt(lhs, _bd(ch["bq"].astype(BF16), lane_lo))
        ch["a_kk"] = jnp.where(strict, a_k[:CHUNK], 0.0)
        ch["a_rk"] = jnp.where(incl, a_k[CHUNK:], 0.0)
        ch["a_kb"] = jnp.where(strict, a_b[:CHUNK], 0.0)
        ch["a_rb"] = jnp.where(incl, a_b[CHUNK:], 0.0)

    for ch in chains:
        ch["t"] = eye_q - ch["a_kb"]
        ch["pw"] = _mm_bd(ch["a_kb"], ch["a_kb"], lane_lo)
        ch["av"] = _mm_bd(jnp.concatenate([ch["a_kk"], ch["a_rk"]], axis=0), ch["vq"], lane_lo)
    for _ in range(4):
        for ch in chains:
            both = _mm_bd(jnp.concatenate([ch["pw"], ch["t"]], axis=0), ch["pw"], lane_lo)
            ch["pw"] = both[:CHUNK]
            ch["t"] = ch["t"] + both[CHUNK:]
    for ch in chains:
        ch["t"] = ch["t"] + _mm_bd(ch["t"], ch["pw"], lane_lo)

    for ch in chains:
        ch["kk_hat"] = _mm_bd(ch["t"], ch["kkq"], lane_lo)
        ch["u0"] = _mm_bd(ch["t"], ch["av"][:CHUNK], lane_lo)
    for ch in chains:
        ch["r_hat"] = ch["rq"] - _mm_bd(ch["a_rb"], ch["kk_hat"], lane_lo)
        ch["y0"] = ch["av"][CHUNK:] - _mm_bd(ch["a_rb"], ch["u0"], lane_lo)
        bt = ch["bhq"].T.astype(BF16)
        kt = ch["khq"].T.astype(BF16)
        m_full = _dot(bt, ch["kk_hat"].astype(BF16))
        g_full = _dot(kt, ch["vq"].astype(BF16)) - _dot(bt, ch["u0"].astype(BF16))
        ch["m_q"] = eye_q * ch["p_end"] - _head_blocks(m_full, head_of_lane)
        ch["g_q"] = _head_blocks(g_full, head_of_lane)

    state = [h_ref[q] for q in range(n_quads)]
    for ch in chains:
        ci, q = ch["ci"], ch["q"]
        out = _mm3_bd(jnp.concatenate([ch["r_hat"], ch["m_q"]], axis=0), state[q], lane_lo)
        y_ref[0, ci * CHUNK:(ci + 1) * CHUNK, q * QUAD:(q + 1) * QUAD] = out[:CHUNK] + ch["y0"]
        state[q] = out[CHUNK:] + ch["g_q"]
    for q in range(n_quads):
        h_ref[q] = state[q]


def _wkv(r, lw, k, v, kk, b):
    batch, seq, width = r.shape
    rows = CHUNK * WKV_CHUNKS_PER_STEP
    spec = pl.BlockSpec((1, rows, width), lambda i, c: (i, c, 0))
    return pl.pallas_call(
        _wkv_kernel,
        out_shape=jax.ShapeDtypeStruct((batch, seq, width), F32),
        grid=(batch, seq // rows),
        in_specs=[spec] * 6,
        out_specs=spec,
        scratch_shapes=[pltpu.VMEM((width // QUAD, HEAD, QUAD), F32)],
        compiler_params=pltpu.CompilerParams(
            dimension_semantics=("arbitrary", "arbitrary"), vmem_limit_bytes=VMEM_LIMIT),
        name="wkv7_chunked",
    )(r, lw, k, v, kk, b)


C_R, C_K, C_V = 0, 512, 1024
C_WA = 1536
C_GV = 1664
N_SHIFTED = 1920
D_RWKV = 512
N_GATE_LORA = 160
SHIFT_PAD = 8


def _rms_mod(x, g, scale, shift):
    ms = jnp.mean(x * x, axis=-1, keepdims=True)
    return x * lax.rsqrt(ms + EPS_RMS) * g * (1.0 + scale) + shift


def _inproj_kernel(has_vres, x_ref, mod_ref, g1_ref, w_ref, mu_ref, vec_ref, up1_ref, up2_ref, ones_ref,
                   *rest):
    if has_vres:
        vfirst_ref, rest = rest[0], rest[1:]
    (r_ref, lw_ref, k_ref, v_ref, kk_ref, b_ref, g_ref, bonus_ref, sh_ref) = rest
    s = pl.program_id(1)
    ts = x_ref.shape[1]
    d = x_ref.shape[2]

    mod = mod_ref[0]
    h = _rms_mod(x_ref[0], g1_ref[...], mod[:, d:2 * d], mod[:, 0:d]).astype(BF16)

    @pl.when(s == 0)
    def _():
        sh_ref[0:SHIFT_PAD, :] = jnp.zeros((SHIFT_PAD, N_SHIFTED), F32)

    @pl.when(s > 0)
    def _():
        sh_ref[0:SHIFT_PAD, :] = sh_ref[ts:ts + SHIFT_PAD, :]

    sh_ref[SHIFT_PAD:SHIFT_PAD + ts, :] = _dot(h, w_ref[...])

    def shifted(c0, c1):
        cur = sh_ref[SHIFT_PAD:SHIFT_PAD + ts, c0:c1]
        prev = sh_ref[SHIFT_PAD - 1:SHIFT_PAD - 1 + ts, c0:c1]
        return cur + (prev - cur) * mu_ref[:, c0:c1]

    w0, a0, v0 = vec_ref[0:1, :], vec_ref[1:2, :], vec_ref[2:3, :]
    k_k, k_a, r_k = vec_ref[3:4, :], vec_ref[4:5, :], vec_ref[5:6, :]

    wa = shifted(C_WA, C_GV)
    lane = lax.broadcasted_iota(jnp.int32, wa.shape, 1)
    wa = jnp.where(lane < 64, jnp.tanh(wa), wa)
    lora1 = _dot(wa.astype(BF16), up1_ref[...])
    gv = shifted(C_GV, N_SHIFTED)
    lane = lax.broadcasted_iota(jnp.int32, gv.shape, 1)
    gv = jnp.where(lane < N_GATE_LORA, _sigmoid(gv), gv)
    lora2 = _dot(gv.astype(BF16), up2_ref[...])

    xw = w0 + lora1[:, :D_RWKV]
    softplus = jnp.maximum(-xw, 0.0) + jnp.log(1.0 + jnp.exp(-jnp.abs(xw)))
    lw_ref[0] = -jnp.exp(-softplus - 0.5)
    a = _sigmoid(a0 + lora1[:, D_RWKV:])
    g_ref[0] = lora2[:, :D_RWKV]

    r = shifted(C_R, C_K)
    k = shifted(C_K, C_V)
    v = shifted(C_V, C_WA)
    if has_vres:
        v = v + (vfirst_ref[0] - v) * _sigmoid(v0 + lora2[:, D_RWKV:])
    r_ref[0] = r
    v_ref[0] = v

    kk = k * k_k
    sq_hi, sq_lo = _split(kk * kk)
    n2 = _dot(sq_hi, ones_ref[...]) + _dot(sq_lo, ones_ref[...])
    kk = kk / jnp.maximum(jnp.sqrt(n2), 1e-12)
    kk_ref[0] = kk
    b_ref[0] = a * kk
    k = k * (1.0 + (a - 1.0) * k_a)
    k_ref[0] = k
    bonus_ref[0] = _dot((r * k * r_k).astype(BF16), ones_ref[...]) * v


def _inproj(x, mod, g1, w, mu, vecs, up1, up2, ones_bd, vfirst):
    batch, seq, d = x.shape
    ts = TOKEN_TILE
    has_vres = vfirst is not None
    tok = lambda n: pl.BlockSpec((1, ts, n), lambda i, s: (i, s, 0))
    full = lambda a: pl.BlockSpec(a.shape, lambda i, s: (0,) * a.ndim)
    in_specs = [tok(d), pl.BlockSpec((1, 1, mod.shape[2]), lambda i, s: (i, 0, 0)),
                full(g1), full(w), full(mu), full(vecs), full(up1), full(up2), full(ones_bd)]
    args = [x, mod, g1, w, mu, vecs, up1, up2, ones_bd]
    if has_vres:
        in_specs.append(tok(D_RWKV))
        args.append(vfirst)
    out_shape = [jax.ShapeDtypeStruct((batch, seq, D_RWKV), F32)] * 8
    out_specs = [tok(D_RWKV)] * 8
    return pl.pallas_call(
        functools.partial(_inproj_kernel, has_vres),
        out_shape=out_shape,
        grid=(batch, seq // ts),
        in_specs=in_specs,
        out_specs=out_specs,
        scratch_shapes=[pltpu.VMEM((ts + SHIFT_PAD, N_SHIFTED), F32)],
        compiler_params=pltpu.CompilerParams(
            dimension_semantics=("arbitrary", "arbitrary"), vmem_limit_bytes=VMEM_LIMIT),
        name="inproj_vres" if has_vres else "inproj",
    )(*args)


def _post_kernel(final, y_ref, g_ref, bonus_ref, x_ref, mod_ref, vec_ref, g1_ref, g2_ref, wgp_ref, avg_ref,
                 poolw_ref, pa_ref, pb_ref, wout_ref, wgu_ref, wd_ref, *rest):
    if final:
        gf_ref, o_ref, halo_ref, acc_ref = rest
    else:
        o_ref, halo_ref, acc_ref = rest
    s = pl.program_id(1)
    ts = x_ref.shape[1]
    d = x_ref.shape[2]
    n_pool = halo_ref.shape[1]
    lnx_g, lnx_b, pool_scale = vec_ref[0:1, :], vec_ref[1:2, :], vec_ref[2:3, :]
    mod = mod_ref[0]
    x = x_ref[0]
    h1 = _rms_mod(x, g1_ref[...], mod[:, d:2 * d], mod[:, 0:d]).astype(BF16)

    y = y_ref[0]
    mean = _dot(y.astype(BF16), avg_ref[...])
    dev = y - mean
    var = _dot((dev * dev).astype(BF16), avg_ref[...])
    ya = (dev * lax.rsqrt(var + EPS_GN) * lnx_g + lnx_b + bonus_ref[0]) * g_ref[0]
    proj_a = _dot(ya.astype(BF16), pa_ref[...])

    @pl.when(s == 0)
    def _():
        halo_ref[0:POOL_HALO, :] = jnp.zeros((POOL_HALO, halo_ref.shape[1]), F32)

    @pl.when(s > 0)
    def _():
        halo_ref[0:POOL_HALO, :] = halo_ref[ts:ts + POOL_HALO, :]

    z = _dot(h1, wgp_ref[:, 0:n_pool])
    halo_ref[POOL_HALO:POOL_HALO + ts, :] = z
    pos = (s * ts + 1 + lax.broadcasted_iota(jnp.int32, (ts, LANES), 0)).astype(F32)
    pooled = []
    for gi, win in enumerate(POOL_WINDOWS):
        acc = halo_ref[:, gi * LANES:(gi + 1) * LANES]
        span = 1
        while span < win:
            acc = acc + pltpu.roll(acc, span, 0)
            span *= 2
        mean_g = acc[POOL_HALO:, :] / jnp.minimum(pos, float(win))
        pooled.append(mean_g - z[:, gi * LANES:(gi + 1) * LANES])
    pooled = jnp.concatenate(pooled, axis=1)
    yb = _dot(pooled.astype(BF16), poolw_ref[...]) * pool_scale
    proj_b = _dot(yb.astype(BF16), pb_ref[...])

    gate_a = _sigmoid(_dot(h1, wgp_ref[:, n_pool:n_pool + d]))
    gate_b = _sigmoid(_dot(h1, wgp_ref[:, n_pool + d:n_pool + 2 * d]))
    mixed = gate_a * proj_a + gate_b * proj_b
    x = x + mod[:, 2 * d:3 * d] * _dot(mixed.astype(BF16), wout_ref[...])

    d_ff = wd_ref.shape[0]
    h = _rms_mod(x, g2_ref[...], mod[:, 4 * d:5 * d], mod[:, 3 * d:4 * d]).astype(BF16)
    for j in range(0, d_ff, FF_BLOCK):
        w = min(FF_BLOCK, d_ff - j)
        gate = _dot(h, wgu_ref[:, j:j + w])
        up = _dot(h, wgu_ref[:, d_ff + j:d_ff + j + w])
        act = (gate * _sigmoid(gate) * up).astype(BF16)
        part = _dot(act, wd_ref[j:j + w, :])
        if j == 0:
            acc_ref[...] = part
        else:
            acc_ref[...] += part
    out = x + mod[:, 5 * d:6 * d] * acc_ref[...]
    if final:
        ms = jnp.mean(out * out, axis=-1, keepdims=True)
        out = out * lax.rsqrt(ms + EPS_RMS) * gf_ref[...]
    o_ref[0] = out


def _post(y, g, bonus, x, mod, vecs, g1, g2, w_gp, avg_bd, pool_bd, proj_a, proj_b, w_out, w_gu, w_down, final_g):
    batch, seq, d = x.shape
    ts = TOKEN_TILE
    final = final_g is not None
    tok = lambda n: pl.BlockSpec((1, ts, n), lambda i, s: (i, s, 0))
    full = lambda a: pl.BlockSpec(a.shape, lambda i, s: (0,) * a.ndim, pipeline_mode=pl.Buffered(1))
    weights = [vecs, g1, g2, w_gp, avg_bd, pool_bd, proj_a, proj_b, w_out, w_gu, w_down]
    if final:
        weights.append(final_g)
    n_pool = pool_bd.shape[0]
    return pl.pallas_call(
        functools.partial(_post_kernel, final),
        out_shape=jax.ShapeDtypeStruct((batch, seq, d), F32),
        grid=(batch, seq // ts),
        in_specs=[tok(D_RWKV), tok(D_RWKV), tok(D_RWKV), tok(d),
                  pl.BlockSpec((1, 1, mod.shape[2]), lambda i, s: (i, 0, 0))] + [full(a) for a in weights],
        out_specs=tok(d),
        scratch_shapes=[pltpu.VMEM((ts + POOL_HALO, n_pool), F32), pltpu.VMEM((ts, d), F32)],
        compiler_params=pltpu.CompilerParams(
            dimension_semantics=("arbitrary", "arbitrary"), vmem_limit_bytes=VMEM_LIMIT),
        name="post_final" if final else "post",
    )(y, g, bonus, x, mod, *weights)


def _block_diag(blocks):
    n = len(blocks)
    rows = []
    for i, blk in enumerate(blocks):
        rows.append(jnp.concatenate(
            [blk if j == i else jnp.zeros((blk.shape[0], blocks[j].shape[1]), blk.dtype) for j in range(n)], axis=1))
    return jnp.concatenate(rows, axis=0)


def _pack_inproj(w_in_l, v_down_l, mu_shift_l, mu_v_l):
    d = w_in_l.shape[0]
    n_shift = mu_shift_l.shape[0]
    n_vl = N_SHIFTED - n_shift - 64
    vd = jnp.zeros((d, n_vl), F32) if v_down_l is None else v_down_l
    mv = jnp.zeros((n_vl,), F32) if mu_v_l is None else mu_v_l
    w = jnp.concatenate([w_in_l[:, :n_shift], vd, jnp.zeros((d, 64), F32)], axis=1)
    mu = jnp.concatenate([mu_shift_l, mv, jnp.zeros((64,), F32)])
    return w.astype(BF16), w_in_l[:, n_shift:].astype(BF16), mu.reshape(1, N_SHIFTED)


def kernel(x, c, ada_w, ada_b, norm1_g, w_in, v_down, mu_shift, mu_v, w0, w_up, a0, a_up, v0, v_up, g_up, k_k, k_a, r_k, lnx_g, lnx_b, pool_w, pool_scale, proj_a, proj_b, w_out, norm2_g, w_gu, w_down, final_g):
    depth = w_in.shape[0]
    d = x.shape[2]
    batch = x.shape[0]
    mods = _modulation(c, ada_w, ada_b).reshape(depth, batch, 1, 6 * d)
    n_heads = D_RWKV // HEAD
    ones_bd = _block_diag([jnp.ones((HEAD, HEAD), BF16)] * n_heads)
    avg_bd = _block_diag([jnp.full((HEAD, HEAD), 1.0 / HEAD, BF16)] * n_heads)
    zeros_r = jnp.zeros((D_RWKV,), F32)

    v_first = None
    for l in range(depth):
        w1, w_gp, mu = _pack_inproj(w_in[l], v_down[l - 1] if l else None, mu_shift[l], mu_v[l - 1] if l else None)
        vecs = jnp.stack([w0[l], a0[l], v0[l - 1] if l else zeros_r, k_k[l], k_a[l], r_k[l].reshape(-1),
                          zeros_r, zeros_r])
        up1 = _block_diag([w_up[l], a_up[l]]).astype(BF16)
        n_vl = v_up.shape[1]
        up2 = jnp.concatenate([
            jnp.concatenate([g_up[l], jnp.zeros((N_GATE_LORA, D_RWKV), F32)], axis=1),
            jnp.concatenate([jnp.zeros((n_vl, D_RWKV), F32), v_up[l - 1] if l else jnp.zeros((n_vl, D_RWKV), F32)], axis=1),
            jnp.zeros((N_SHIFTED - C_GV - N_GATE_LORA - n_vl, 2 * D_RWKV), F32)], axis=0).astype(BF16)
        g1 = norm1_g[l].reshape(1, d)
        r, lw, k, v, kk, b, g, bonus = _inproj(x, mods[l], g1, w1, mu, vecs, up1, up2, ones_bd, v_first)
        if l == 0:
            v_first = v
        y = _wkv(r, lw, k, v, kk, b)
        mix_vecs = jnp.stack([lnx_g[l], lnx_b[l], pool_scale[l]] + [zeros_r] * 5)
        pool_bd = _block_diag([pool_w[l, i] for i in range(pool_w.shape[1])]).astype(BF16)
        x = _post(y, g, bonus, x, mods[l], mix_vecs, g1, norm2_g[l].reshape(1, d), w_gp, avg_bd, pool_bd,
                  proj_a[l].astype(BF16), proj_b[l].astype(BF16), w_out[l].astype(BF16),
                  w_gu[l].astype(BF16), w_down[l].astype(BF16),
                  final_g.reshape(1, d) if l == depth - 1 else None)
    return x
```

```python
import functools

import jax
import jax.numpy as jnp
from jax import lax
from jax.experimental import pallas as pl
from jax.experimental.pallas import tpu as pltpu

F32 = jnp.float32
BF16 = jnp.bfloat16

HEAD = 64
CHUNK = 64
QUAD = 4 * HEAD
LANES = 128
EPS_RMS = 1e-6
EPS_GN = 64e-5
POOL_WINDOWS = (2, 4, 8, 16)
POOL_HALO = 16
TOKEN_TILE = 256
FF_BLOCK = 512
VMEM_LIMIT = 56 * 1024 * 1024


def _dot(a, b):
    return jnp.dot(a, b, preferred_element_type=F32)


def _dot_nt(a, b):
    return lax.dot_general(a, b, (((1,), (1,)), ((), ())), preferred_element_type=F32)


def _dot_tn(a, b):
    return lax.dot_general(a, b, (((0,), (0,)), ((), ())), preferred_element_type=F32)


def _split(x):
    hi = x.astype(BF16)
    lo = (x - hi.astype(F32)).astype(BF16)
    return hi, lo


def _sigmoid(x):
    return 1.0 / (1.0 + jnp.exp(-x))


def _mod_kernel(c_ref, w_ref, b_ref, o_ref):
    c = c_ref[...]
    c_act = (c * _sigmoid(c)).astype(BF16)
    o_ref[0] = _dot(c_act, w_ref[0].astype(BF16)) + b_ref[0]


def _modulation(c, ada_w, ada_b):
    depth, d, n_mod = ada_w.shape
    batch = c.shape[0]
    nb = 1024
    return pl.pallas_call(
        _mod_kernel,
        out_shape=jax.ShapeDtypeStruct((depth, batch, n_mod), F32),
        grid=(depth, n_mod // nb),
        in_specs=[
            pl.BlockSpec((batch, d), lambda l, j: (0, 0)),
            pl.BlockSpec((1, d, nb), lambda l, j: (l, 0, j)),
            pl.BlockSpec((1, 1, nb), lambda l, j: (l, 0, j)),
        ],
        out_specs=pl.BlockSpec((1, batch, nb), lambda l, j: (l, 0, j)),
        name="adaln_mod",
    )(c, ada_w, ada_b.reshape(depth, 1, n_mod))


def _bd(xq, lane_lo):
    z = jnp.zeros((CHUNK, LANES), xq.dtype)
    zero = jnp.zeros((), xq.dtype)
    x0 = xq[:, :LANES]
    x1 = xq[:, LANES:]
    rows = [
        jnp.concatenate([jnp.where(lane_lo, x0, zero), z], axis=1),
        jnp.concatenate([jnp.where(lane_lo, zero, x0), z], axis=1),
        jnp.concatenate([z, jnp.where(lane_lo, x1, zero)], axis=1),
        jnp.concatenate([z, jnp.where(lane_lo, zero, x1)], axis=1),
    ]
    return jnp.concatenate(rows, axis=0)


def _mm_bd(a, xq, lane_lo):
    return _dot(a.astype(BF16), _bd(xq.astype(BF16), lane_lo))


def _mm3_bd(a, xq, lane_lo):
    m = a.shape[0]
    ah, al = _split(a)
    xh, xl = _split(xq)
    both = _dot(jnp.concatenate([ah, al], axis=0), _bd(xh, lane_lo))
    return both[:m] + (both[m:] + _dot(ah, _bd(xl, lane_lo)))


def _head_transpose(xq):
    halves = []
    for i in range(QUAD // LANES):
        t = xq[:, i * LANES:(i + 1) * LANES].T
        halves.append(jnp.concatenate([t[:HEAD], t[HEAD:]], axis=1))
    return jnp.concatenate(halves, axis=1)


WKV_CHUNKS_PER_STEP = 8


def _wkv_kernel(r_ref, lw_ref, k_ref, v_ref, kk_ref, b_ref, y_ref, h_ref):
    @pl.when(pl.program_id(1) == 0)
    def _():
        h_ref[...] = jnp.zeros_like(h_ref)

    row = lax.broadcasted_iota(jnp.int32, (CHUNK, QUAD), 0)
    col = lax.broadcasted_iota(jnp.int32, (CHUNK, QUAD), 1) % HEAD
    strict = col < row
    incl = col <= row
    eye_q = (col == row).astype(F32)
    lane_lo = lax.broadcasted_iota(jnp.int32, (CHUNK, LANES), 1) < HEAD
    tri = (lax.broadcasted_iota(jnp.int32, (CHUNK, CHUNK), 1)
           <= lax.broadcasted_iota(jnp.int32, (CHUNK, CHUNK), 0)).astype(BF16)

    n_chunks = r_ref.shape[1] // CHUNK
    n_quads = r_ref.shape[2] // QUAD
    state = [h_ref[q] for q in range(n_quads)]

    def chunk_inputs(ci):
        rows = slice(ci * CHUNK, (ci + 1) * CHUNK)
        lw = lw_ref[0, rows, :]
        lw_hi, lw_lo = _split(lw)
        cum = _dot(tri, lw_hi) + _dot(tri, lw_lo)
        cum_end = cum[CHUNK - 1:CHUNK, :]
        e_out = jnp.exp(-cum)
        e_tail = jnp.exp(cum_end - cum)
        k_raw = k_ref[0, rows, :]
        b_raw = b_ref[0, rows, :]
        return dict(p_end=jnp.exp(cum_end),
                    r=r_ref[0, rows, :] * jnp.exp(cum),
                    kk=kk_ref[0, rows, :] * jnp.exp(cum - lw),
                    k=k_raw * e_out, b=b_raw * e_out, k_hat=k_raw * e_tail, b_hat=b_raw * e_tail,
                    v=v_ref[0, rows, :])

    def chain(ci, q, inp):
        sl = slice(q * QUAD, (q + 1) * QUAD)
        rq, kkq, vq = inp["r"][:, sl], inp["kk"][:, sl], inp["v"][:, sl]
        lhs = jnp.concatenate([kkq, rq], axis=0)
        a_k = _mm_bd(lhs, _head_transpose(inp["k"][:, sl]), lane_lo)
        a_b = _mm_bd(lhs, _head_transpose(inp["b"][:, sl]), lane_lo)
        a_kk = jnp.where(strict, a_k[:CHUNK], 0.0)
        a_rk = jnp.where(incl, a_k[CHUNK:], 0.0)
        a_kb = jnp.where(strict, a_b[:CHUNK], 0.0)
        a_rb = jnp.where(incl, a_b[CHUNK:], 0.0)
        bt = _head_transpose(inp["b_hat"][:, sl])
        kt = _head_transpose(inp["k_hat"][:, sl])
        yield
        t_inv = eye_q - a_kb
        pw = _mm_bd(a_kb, a_kb, lane_lo)
        av = _mm_bd(jnp.concatenate([a_kk, a_rk, kt], axis=0), vq, lane_lo)
        akk_v, ark_v, kt_v = av[:CHUNK], av[CHUNK:2 * CHUNK], av[2 * CHUNK:]
        yield
        for _ in range(4):
            both = _mm_bd(jnp.concatenate([pw, t_inv], axis=0), pw, lane_lo)
            pw = both[:CHUNK]
            t_inv = t_inv + both[CHUNK:]
            yield
        t_inv = t_inv + _mm_bd(t_inv, pw, lane_lo)
        yield
        kk_hat = _mm_bd(t_inv, kkq, lane_lo)
        u0 = _mm_bd(t_inv, akk_v, lane_lo)
        yield
        lhs2 = jnp.concatenate([a_rb, bt], axis=0)
        p1 = _mm_bd(lhs2, kk_hat, lane_lo)
        p2 = _mm_bd(lhs2, u0, lane_lo)
        r_hat = rq - p1[:CHUNK]
        y0 = ark_v - p2[:CHUNK]
        m_q = eye_q * inp["p_end"][:, sl] - p1[CHUNK:]
        g_q = kt_v - p2[CHUNK:]
        yield
        out = _mm3_bd(jnp.concatenate([r_hat, m_q], axis=0), state[q], lane_lo)
        y_ref[0, ci * CHUNK:(ci + 1) * CHUNK, sl] = out[:CHUNK] + y0
        state[q] = out[CHUNK:] + g_q
        yield

    gens = [[chain(ci, q, inp) for q in range(n_quads)] for ci, inp in
            ((ci, chunk_inputs(ci)) for ci in range(n_chunks))]
    n_stages = 10
    for tick in range(n_stages + n_chunks - 1):
        for ci in range(n_chunks):
            if 0 <= tick - ci < n_stages:
                for gen in gens[ci]:
                    next(gen)
    for q in range(n_quads):
        h_ref[q] = state[q]


def _wkv(r, lw, k, v, kk, b):
    batch, seq, width = r.shape
    rows = CHUNK * WKV_CHUNKS_PER_STEP
    spec = pl.BlockSpec((1, rows, width), lambda i, c: (i, c, 0))
    return pl.pallas_call(
        _wkv_kernel,
        out_shape=jax.ShapeDtypeStruct((batch, seq, width), F32),
        grid=(batch, seq // rows),
        in_specs=[spec] * 6,
        out_specs=spec,
        scratch_shapes=[pltpu.VMEM((width // QUAD, HEAD, QUAD), F32)],
        compiler_params=pltpu.CompilerParams(
            dimension_semantics=("arbitrary", "arbitrary"), vmem_limit_bytes=VMEM_LIMIT),
        name="wkv7_chunked",
    )(r, lw, k, v, kk, b)


C_R, C_K, C_V = 0, 512, 1024
C_WA = 1536
C_GV = 1664
N_SHIFTED = 1920
D_RWKV = 512
N_GATE_LORA = 160
SHIFT_PAD = 8


def _rms_mod(x, g, scale, shift):
    ms = jnp.mean(x * x, axis=-1, keepdims=True)
    return x * lax.rsqrt(ms + EPS_RMS) * g * (1.0 + scale) + shift


def _inproj_kernel(has_vres, x_ref, mod_ref, g1_ref, w_ref, mu_ref, vec_ref, up1_ref, up2_ref, ones_ref,
                   *rest):
    if has_vres:
        vfirst_ref, rest = rest[0], rest[1:]
    (r_ref, lw_ref, k_ref, v_ref, kk_ref, b_ref, g_ref, bonus_ref, sh_ref) = rest
    s = pl.program_id(1)
    ts = x_ref.shape[1]
    d = x_ref.shape[2]

    mod = mod_ref[0]
    h = _rms_mod(x_ref[0], g1_ref[...], mod[:, d:2 * d], mod[:, 0:d]).astype(BF16)

    @pl.when(s == 0)
    def _():
        sh_ref[0:SHIFT_PAD, :] = jnp.zeros((SHIFT_PAD, N_SHIFTED), F32)

    @pl.when(s > 0)
    def _():
        sh_ref[0:SHIFT_PAD, :] = sh_ref[ts:ts + SHIFT_PAD, :]

    sh_ref[SHIFT_PAD:SHIFT_PAD + ts, :] = _dot(h, w_ref[...])

    def shifted(c0, c1):
        cur = sh_ref[SHIFT_PAD:SHIFT_PAD + ts, c0:c1]
        prev = sh_ref[SHIFT_PAD - 1:SHIFT_PAD - 1 + ts, c0:c1]
        return cur + (prev - cur) * mu_ref[:, c0:c1]

    w0, a0, v0 = vec_ref[0:1, :], vec_ref[1:2, :], vec_ref[2:3, :]
    k_k, k_a, r_k = vec_ref[3:4, :], vec_ref[4:5, :], vec_ref[5:6, :]

    wa = shifted(C_WA, C_GV)
    lane = lax.broadcasted_iota(jnp.int32, wa.shape, 1)
    wa = jnp.where(lane < 64, jnp.tanh(wa), wa)
    lora1 = _dot(wa.astype(BF16), up1_ref[...])
    gv = shifted(C_GV, N_SHIFTED)
    lane = lax.broadcasted_iota(jnp.int32, gv.shape, 1)
    gv = jnp.where(lane < N_GATE_LORA, _sigmoid(gv), gv)
    lora2 = _dot(gv.astype(BF16), up2_ref[...])

    xw = w0 + lora1[:, :D_RWKV]
    softplus = jnp.maximum(-xw, 0.0) + jnp.log(1.0 + jnp.exp(-jnp.abs(xw)))
    lw_ref[0] = -jnp.exp(-softplus - 0.5)
    a = _sigmoid(a0 + lora1[:, D_RWKV:])
    g_ref[0] = lora2[:, :D_RWKV]

    r = shifted(C_R, C_K)
    k = shifted(C_K, C_V)
    v = shifted(C_V, C_WA)
    if has_vres:
        v = v + (vfirst_ref[0] - v) * _sigmoid(v0 + lora2[:, D_RWKV:])
    r_ref[0] = r
    v_ref[0] = v

    kk = k * k_k
    sq_hi, sq_lo = _split(kk * kk)
    n2 = _dot(sq_hi, ones_ref[...]) + _dot(sq_lo, ones_ref[...])
    kk = kk / jnp.maximum(jnp.sqrt(n2), 1e-12)
    kk_ref[0] = kk
    b_ref[0] = a * kk
    k = k * (1.0 + (a - 1.0) * k_a)
    k_ref[0] = k
    bonus_ref[0] = _dot((r * k * r_k).astype(BF16), ones_ref[...]) * v


def _inproj(x, mod, g1, w, mu, vecs, up1, up2, ones_bd, vfirst):
    batch, seq, d = x.shape
    ts = TOKEN_TILE
    has_vres = vfirst is not None
    tok = lambda n: pl.BlockSpec((1, ts, n), lambda i, s: (i, s, 0))
    full = lambda a: pl.BlockSpec(a.shape, lambda i, s: (0,) * a.ndim)
    in_specs = [tok(d), pl.BlockSpec((1, 1, mod.shape[2]), lambda i, s: (i, 0, 0)),
                full(g1), full(w), full(mu), full(vecs), full(up1), full(up2), full(ones_bd)]
    args = [x, mod, g1, w, mu, vecs, up1, up2, ones_bd]
    if has_vres:
        in_specs.append(tok(D_RWKV))
        args.append(vfirst)
    out_shape = [jax.ShapeDtypeStruct((batch, seq, D_RWKV), F32)] * 8
    out_specs = [tok(D_RWKV)] * 8
    return pl.pallas_call(
        functools.partial(_inproj_kernel, has_vres),
        out_shape=out_shape,
        grid=(batch, seq // ts),
        in_specs=in_specs,
        out_specs=out_specs,
        scratch_shapes=[pltpu.VMEM((ts + SHIFT_PAD, N_SHIFTED), F32)],
        compiler_params=pltpu.CompilerParams(
            dimension_semantics=("arbitrary", "arbitrary"), vmem_limit_bytes=VMEM_LIMIT),
        name="inproj_vres" if has_vres else "inproj",
    )(*args)


def _post_kernel(final, y_ref, g_ref, bonus_ref, x_ref, mod_ref, vec_ref, g1_ref, g2_ref, wgp_ref, avg_ref,
                 poolw_ref, pa_ref, pb_ref, wout_ref, wgu_ref, wd_ref, *rest):
    if final:
        gf_ref, o_ref, halo_ref, act_ref = rest
    else:
        o_ref, halo_ref, act_ref = rest
    s = pl.program_id(1)
    ts = x_ref.shape[1]
    d = x_ref.shape[2]
    n_pool = halo_ref.shape[1]
    lnx_g, lnx_b, pool_scale = vec_ref[0:1, :], vec_ref[1:2, :], vec_ref[2:3, :]
    mod = mod_ref[0]
    x = x_ref[0]
    h1 = _rms_mod(x, g1_ref[...], mod[:, d:2 * d], mod[:, 0:d]).astype(BF16)

    @pl.when(s == 0)
    def _():
        halo_ref[0:POOL_HALO, :] = jnp.zeros((POOL_HALO, halo_ref.shape[1]), F32)

    @pl.when(s > 0)
    def _():
        halo_ref[0:POOL_HALO, :] = halo_ref[ts:ts + POOL_HALO, :]

    y = y_ref[0]
    z = _dot(h1, wgp_ref[:, 0:n_pool])
    mean = _dot(y.astype(BF16), avg_ref[...])
    gate_a = _dot(h1, wgp_ref[:, n_pool:n_pool + d])
    dev = y - mean
    var = _dot((dev * dev).astype(BF16), avg_ref[...])
    gate_b = _dot(h1, wgp_ref[:, n_pool + d:n_pool + 2 * d])
    halo_ref[POOL_HALO:POOL_HALO + ts, :] = z
    pos = (s * ts + 1 + lax.broadcasted_iota(jnp.int32, (ts, LANES), 0)).astype(F32)
    pooled = []
    for gi, win in enumerate(POOL_WINDOWS):
        acc = halo_ref[:, gi * LANES:(gi + 1) * LANES]
        span = 1
        while span < win:
            acc = acc + pltpu.roll(acc, span, 0)
            span *= 2
        mean_g = acc[POOL_HALO:, :] / jnp.minimum(pos, float(win))
        pooled.append(mean_g - z[:, gi * LANES:(gi + 1) * LANES])
    pooled = jnp.concatenate(pooled, axis=1)
    yb = _dot(pooled.astype(BF16), poolw_ref[...]) * pool_scale
    ya = (dev * lax.rsqrt(var + EPS_GN) * lnx_g + lnx_b + bonus_ref[0]) * g_ref[0]
    proj_a = _dot(ya.astype(BF16), pa_ref[...])
    proj_b = _dot(yb.astype(BF16), pb_ref[...])
    mixed = _sigmoid(gate_a) * proj_a + _sigmoid(gate_b) * proj_b
    x = x + mod[:, 2 * d:3 * d] * _dot(mixed.astype(BF16), wout_ref[...])

    d_ff = wd_ref.shape[0]
    h = _rms_mod(x, g2_ref[...], mod[:, 4 * d:5 * d], mod[:, 3 * d:4 * d]).astype(BF16)
    for j in range(0, d_ff, FF_BLOCK):
        w = min(FF_BLOCK, d_ff - j)
        gate = _dot(h, wgu_ref[:, j:j + w])
        up = _dot(h, wgu_ref[:, d_ff + j:d_ff + j + w])
        act_ref[:, j:j + w] = (gate * _sigmoid(gate) * up).astype(BF16)
    out = x + mod[:, 5 * d:6 * d] * _dot(act_ref[...], wd_ref[...])
    if final:
        ms = jnp.mean(out * out, axis=-1, keepdims=True)
        out = out * lax.rsqrt(ms + EPS_RMS) * gf_ref[...]
    o_ref[0] = out


def _post(y, g, bonus, x, mod, vecs, g1, g2, w_gp, avg_bd, pool_bd, proj_a, proj_b, w_out, w_gu, w_down, final_g):
    batch, seq, d = x.shape
    ts = TOKEN_TILE
    final = final_g is not None
    tok = lambda n: pl.BlockSpec((1, ts, n), lambda i, s: (i, s, 0))
    full = lambda a: pl.BlockSpec(a.shape, lambda i, s: (0,) * a.ndim, pipeline_mode=pl.Buffered(1))
    weights = [vecs, g1, g2, w_gp, avg_bd, pool_bd, proj_a, proj_b, w_out, w_gu, w_down]
    if final:
        weights.append(final_g)
    n_pool = pool_bd.shape[0]
    return pl.pallas_call(
        functools.partial(_post_kernel, final),
        out_shape=jax.ShapeDtypeStruct((batch, seq, d), F32),
        grid=(batch, seq // ts),
        in_specs=[tok(D_RWKV), tok(D_RWKV), tok(D_RWKV), tok(d),
                  pl.BlockSpec((1, 1, mod.shape[2]), lambda i, s: (i, 0, 0))] + [full(a) for a in weights],
        out_specs=tok(d),
        scratch_shapes=[pltpu.VMEM((ts + POOL_HALO, n_pool), F32), pltpu.VMEM((ts, w_down.shape[0]), BF16)],
        compiler_params=pltpu.CompilerParams(
            dimension_semantics=("arbitrary", "arbitrary"), vmem_limit_bytes=VMEM_LIMIT),
        name="post_final" if final else "post",
    )(y, g, bonus, x, mod, *weights)


def _block_diag(blocks):
    n = len(blocks)
    rows = []
    for i, blk in enumerate(blocks):
        rows.append(jnp.concatenate(
            [blk if j == i else jnp.zeros((blk.shape[0], blocks[j].shape[1]), blk.dtype) for j in range(n)], axis=1))
    return jnp.concatenate(rows, axis=0)


def _pack_inproj(w_in_l, v_down_l, mu_shift_l, mu_v_l):
    d = w_in_l.shape[0]
    n_shift = mu_shift_l.shape[0]
    n_vl = N_SHIFTED - n_shift - 64
    vd = jnp.zeros((d, n_vl), F32) if v_down_l is None else v_down_l
    mv = jnp.zeros((n_vl,), F32) if mu_v_l is None else mu_v_l
    w = jnp.concatenate([w_in_l[:, :n_shift], vd, jnp.zeros((d, 64), F32)], axis=1)
    mu = jnp.concatenate([mu_shift_l, mv, jnp.zeros((64,), F32)])
    return w.astype(BF16), w_in_l[:, n_shift:].astype(BF16), mu.reshape(1, N_SHIFTED)


def kernel(x, c, ada_w, ada_b, norm1_g, w_in, v_down, mu_shift, mu_v, w0, w_up, a0, a_up, v0, v_up, g_up, k_k, k_a, r_k, lnx_g, lnx_b, pool_w, pool_scale, proj_a, proj_b, w_out, norm2_g, w_gu, w_down, final_g):
    depth = w_in.shape[0]
    d = x.shape[2]
    batch = x.shape[0]
    mods = _modulation(c, ada_w, ada_b).reshape(depth, batch, 1, 6 * d)
    n_heads = D_RWKV // HEAD
    ones_bd = _block_diag([jnp.ones((HEAD, HEAD), BF16)] * n_heads)
    avg_bd = _block_diag([jnp.full((HEAD, HEAD), 1.0 / HEAD, BF16)] * n_heads)
    zeros_r = jnp.zeros((D_RWKV,), F32)

    v_first = None
    for l in range(depth):
        w1, w_gp, mu = _pack_inproj(w_in[l], v_down[l - 1] if l else None, mu_shift[l], mu_v[l - 1] if l else None)
        vecs = jnp.stack([w0[l], a0[l], v0[l - 1] if l else zeros_r, k_k[l], k_a[l], r_k[l].reshape(-1),
                          zeros_r, zeros_r])
        up1 = _block_diag([w_up[l], a_up[l]]).astype(BF16)
        n_vl = v_up.shape[1]
        up2 = jnp.concatenate([
            jnp.concatenate([g_up[l], jnp.zeros((N_GATE_LORA, D_RWKV), F32)], axis=1),
            jnp.concatenate([jnp.zeros((n_vl, D_RWKV), F32), v_up[l - 1] if l else jnp.zeros((n_vl, D_RWKV), F32)], axis=1),
            jnp.zeros((N_SHIFTED - C_GV - N_GATE_LORA - n_vl, 2 * D_RWKV), F32)], axis=0).astype(BF16)
        g1 = norm1_g[l].reshape(1, d)
        r, lw, k, v, kk, b, g, bonus = _inproj(x, mods[l], g1, w1, mu, vecs, up1, up2, ones_bd, v_first)
        if l == 0:
            v_first = v
        y = _wkv(r, lw, k, v, kk, b)
        mix_vecs = jnp.stack([lnx_g[l], lnx_b[l], pool_scale[l]] + [zeros_r] * 5)
        pool_bd = _block_diag([pool_w[l, i] for i in range(pool_w.shape[1])]).astype(BF16)
        x = _post(y, g, bonus, x, mods[l], mix_vecs, g1, norm2_g[l].reshape(1, d), w_gp, avg_bd, pool_bd,
                  proj_a[l].astype(BF16), proj_b[l].astype(BF16), w_out[l].astype(BF16),
                  w_gu[l].astype(BF16), w_down[l].astype(BF16),
                  final_g.reshape(1, d) if l == depth - 1 else None)
    return x
```

```python
import functools
from typing import Any, Callable, NamedTuple, Sequence

import jax
import jax.numpy as jnp
from jax import lax
from jax.experimental import pallas as pl
from jax.experimental.pallas import tpu as pltpu

F32 = jnp.float32
BF16 = jnp.bfloat16

HEAD = 64
CHUNK = 64
QUAD = 4 * HEAD
LANES = 128
D_RWKV = 512
EPS_RMS = 1e-6
EPS_GN = 64e-5
POOL_WINDOWS = (2, 4, 8, 16)
POOL_HALO = 16
TOKEN_TILE = 256
FF_BLOCK = 256
DOWN_BLOCK = 256
VMEM_LIMIT = 60 * 1024 * 1024

C_R, C_K, C_V = 0, 512, 1024
C_WA = 1536
C_GV = 1664
N_SHIFTED = 1920
N_GATE_LORA = 160
SHIFT_PAD = 8


def _dot(a, b):
    return jnp.dot(a, b, preferred_element_type=F32)


def _split(x):
    hi = x.astype(BF16)
    lo = (x - hi.astype(F32)).astype(BF16)
    return hi, lo


def _sigmoid(x):
    return 1.0 / (1.0 + jnp.exp(-x))


def _rms_mod(x, g, scale, shift):
    ms = jnp.mean(x * x, axis=-1, keepdims=True)
    return x * lax.rsqrt(ms + EPS_RMS) * g * (1.0 + scale) + shift


def _mod_kernel(c_ref, w_ref, b_ref, o_ref):
    c = c_ref[...]
    c_act = (c * _sigmoid(c)).astype(BF16)
    o_ref[0] = _dot(c_act, w_ref[0].astype(BF16)) + b_ref[0]


def _modulation(c, ada_w, ada_b):
    depth, d, n_mod = ada_w.shape
    batch = c.shape[0]
    nb = 1024
    return pl.pallas_call(
        _mod_kernel,
        out_shape=jax.ShapeDtypeStruct((depth, batch, n_mod), F32),
        grid=(depth, n_mod // nb),
        in_specs=[
            pl.BlockSpec((batch, d), lambda l, j: (0, 0)),
            pl.BlockSpec((1, d, nb), lambda l, j: (l, 0, j)),
            pl.BlockSpec((1, 1, nb), lambda l, j: (l, 0, j)),
        ],
        out_specs=pl.BlockSpec((1, batch, nb), lambda l, j: (l, 0, j)),
        name="adaln_mod",
    )(c, ada_w, ada_b.reshape(depth, 1, n_mod))


def _bd(xq, lane_lo):
    z = jnp.zeros((CHUNK, LANES), xq.dtype)
    zero = jnp.zeros((), xq.dtype)
    x0 = xq[:, :LANES]
    x1 = xq[:, LANES:]
    rows = [
        jnp.concatenate([jnp.where(lane_lo, x0, zero), z], axis=1),
        jnp.concatenate([jnp.where(lane_lo, zero, x0), z], axis=1),
        jnp.concatenate([z, jnp.where(lane_lo, x1, zero)], axis=1),
        jnp.concatenate([z, jnp.where(lane_lo, zero, x1)], axis=1),
    ]
    return jnp.concatenate(rows, axis=0)


def _mm_bd(a, xq, lane_lo):
    return _dot(a.astype(BF16), _bd(xq.astype(BF16), lane_lo))


def _mm3_bd(a, xq, lane_lo):
    m = a.shape[0]
    ah, al = _split(a)
    xh, xl = _split(xq)
    both = _dot(jnp.concatenate([ah, al], axis=0), _bd(xh, lane_lo))
    return both[:m] + (both[m:] + _dot(ah, _bd(xl, lane_lo)))


def _head_transpose(xq):
    halves = []
    for i in range(QUAD // LANES):
        t = xq[:, i * LANES:(i + 1) * LANES].T
        halves.append(jnp.concatenate([t[:HEAD], t[HEAD:]], axis=1))
    return jnp.concatenate(halves, axis=1)


def _wkv_stages(ops_ref, y_ref, h_ref):
    row = lax.broadcasted_iota(jnp.int32, (CHUNK, QUAD), 0)
    col = lax.broadcasted_iota(jnp.int32, (CHUNK, QUAD), 1) % HEAD
    strict = col < row
    incl = col <= row
    eye_q = (col == row).astype(F32)
    lane_lo = lax.broadcasted_iota(jnp.int32, (CHUNK, LANES), 1) < HEAD
    tri = (lax.broadcasted_iota(jnp.int32, (CHUNK, CHUNK), 1)
           <= lax.broadcasted_iota(jnp.int32, (CHUNK, CHUNK), 0)).astype(BF16)

    n_chunks = ops_ref.shape[1] // CHUNK
    n_quads = ops_ref.shape[2] // QUAD
    state = [h_ref[q] for q in range(n_quads)]

    def chunk_inputs(ci):
        rows = slice(ci * CHUNK, (ci + 1) * CHUNK)
        lw = ops_ref[1, rows, :]
        lw_hi, lw_lo = _split(lw)
        cum = _dot(tri, lw_hi) + _dot(tri, lw_lo)
        cum_end = cum[CHUNK - 1:CHUNK, :]
        e_out = jnp.exp(-cum)
        e_tail = jnp.exp(cum_end - cum)
        k_raw = ops_ref[2, rows, :]
        b_raw = ops_ref[5, rows, :]
        return dict(p_end=jnp.exp(cum_end),
                    r=ops_ref[0, rows, :] * jnp.exp(cum),
                    kk=ops_ref[4, rows, :] * jnp.exp(cum - lw),
                    k=k_raw * e_out, b=b_raw * e_out, k_hat=k_raw * e_tail, b_hat=b_raw * e_tail,
                    v=ops_ref[3, rows, :])

    def chain(ci, q, inp):
        sl = slice(q * QUAD, (q + 1) * QUAD)
        rq, kkq, vq = inp["r"][:, sl], inp["kk"][:, sl], inp["v"][:, sl]
        lhs = jnp.concatenate([kkq, rq], axis=0)
        a_k = _mm_bd(lhs, _head_transpose(inp["k"][:, sl]), lane_lo)
        a_b = _mm_bd(lhs, _head_transpose(inp["b"][:, sl]), lane_lo)
        a_kk = jnp.where(strict, a_k[:CHUNK], 0.0)
        a_rk = jnp.where(incl, a_k[CHUNK:], 0.0)
        a_kb = jnp.where(strict, a_b[:CHUNK], 0.0)
        a_rb = jnp.where(incl, a_b[CHUNK:], 0.0)
        bt = _head_transpose(inp["b_hat"][:, sl])
        kt = _head_transpose(inp["k_hat"][:, sl])
        yield
        t_inv = eye_q - a_kb
        pw = _mm_bd(a_kb, a_kb, lane_lo)
        av = _mm_bd(jnp.concatenate([a_kk, a_rk, kt], axis=0), vq, lane_lo)
        akk_v, ark_v, kt_v = av[:CHUNK], av[CHUNK:2 * CHUNK], av[2 * CHUNK:]
        yield
        for _ in range(4):
            both = _mm_bd(jnp.concatenate([pw, t_inv], axis=0), pw, lane_lo)
            pw = both[:CHUNK]
            t_inv = t_inv + both[CHUNK:]
            yield
        t_inv = t_inv + _mm_bd(t_inv, pw, lane_lo)
        yield
        kk_hat = _mm_bd(t_inv, kkq, lane_lo)
        u0 = _mm_bd(t_inv, akk_v, lane_lo)
        yield
        lhs2 = jnp.concatenate([a_rb, bt], axis=0)
        p1 = _mm_bd(lhs2, kk_hat, lane_lo)
        p2 = _mm_bd(lhs2, u0, lane_lo)
        r_hat = rq - p1[:CHUNK]
        y0 = ark_v - p2[:CHUNK]
        m_q = eye_q * inp["p_end"][:, sl] - p1[CHUNK:]
        g_q = kt_v - p2[CHUNK:]
        yield
        out = _mm3_bd(jnp.concatenate([r_hat, m_q], axis=0), state[q], lane_lo)
        y_ref[0, ci * CHUNK:(ci + 1) * CHUNK, sl] = out[:CHUNK] + y0
        state[q] = out[CHUNK:] + g_q
        yield

    gens = [[chain(ci, q, inp) for q in range(n_quads)] for ci, inp in
            ((ci, chunk_inputs(ci)) for ci in range(n_chunks))]
    yield 600
    n_stages = 10
    for tick in range(n_stages + n_chunks - 1):
        active = 0
        for ci in range(n_chunks):
            if 0 <= tick - ci < n_stages:
                active += 1
                for gen in gens[ci]:
                    next(gen)
        yield 110 * active
    for q in range(n_quads):
        h_ref[q] = state[q]


def _front_stages(has_vres, in_refs, out_refs, scr_refs):
    (x_ref, mod_ref, g1_ref, w_ref, mu_ref, vec_ref, up1_ref, up2_ref, ones_ref) = in_refs[:9]
    vfirst_ref = in_refs[9] if has_vres else None
    if has_vres:
        y_ref, g_ref, bonus_ref = out_refs
        vout_ref = None
    else:
        y_ref, g_ref, bonus_ref, vout_ref = out_refs
    sh_ref, ops_ref, h_ref = scr_refs
    s = pl.program_id(1)
    ts = x_ref.shape[1]
    d = x_ref.shape[2]

    @pl.when(s == 0)
    def _():
        sh_ref[0:SHIFT_PAD, :] = jnp.zeros((SHIFT_PAD, N_SHIFTED), F32)
        h_ref[...] = jnp.zeros_like(h_ref)

    @pl.when(s > 0)
    def _():
        sh_ref[0:SHIFT_PAD, :] = sh_ref[ts:ts + SHIFT_PAD, :]

    mod = mod_ref[0]
    h = _rms_mod(x_ref[0], g1_ref[...], mod[:, d:2 * d], mod[:, 0:d]).astype(BF16)
    yield 300
    for c0 in range(0, N_SHIFTED, 512):
        c1 = min(c0 + 512, N_SHIFTED)
        sh_ref[SHIFT_PAD:SHIFT_PAD + ts, c0:c1] = _dot(h, w_ref[:, c0:c1])
        yield 600

    def shifted(c0, c1):
        cur = sh_ref[SHIFT_PAD:SHIFT_PAD + ts, c0:c1]
        prev = sh_ref[SHIFT_PAD - 1:SHIFT_PAD - 1 + ts, c0:c1]
        return cur + (prev - cur) * mu_ref[:, c0:c1]

    w0, a0, v0 = vec_ref[0:1, :], vec_ref[1:2, :], vec_ref[2:3, :]
    k_k, k_a, r_k = vec_ref[3:4, :], vec_ref[4:5, :], vec_ref[5:6, :]

    wa = shifted(C_WA, C_GV)
    lane = lax.broadcasted_iota(jnp.int32, wa.shape, 1)
    wa = jnp.where(lane < 64, jnp.tanh(wa), wa)
    lora1 = _dot(wa.astype(BF16), up1_ref[...])
    gv = shifted(C_GV, N_SHIFTED)
    lane = lax.broadcasted_iota(jnp.int32, gv.shape, 1)
    gv = jnp.where(lane < N_GATE_LORA, _sigmoid(gv), gv)
    lora2 = _dot(gv.astype(BF16), up2_ref[...])
    yield 400

    xw = w0 + lora1[:, :D_RWKV]
    softplus = jnp.maximum(-xw, 0.0) + jnp.log(1.0 + jnp.exp(-jnp.abs(xw)))
    ops_ref[1] = -jnp.exp(-softplus - 0.5)
    a = _sigmoid(a0 + lora1[:, D_RWKV:])
    g_ref[0] = lora2[:, :D_RWKV]
    yield 500

    r = shifted(C_R, C_K)
    v = shifted(C_V, C_WA)
    if has_vres:
        v = v + (vfirst_ref[0] - v) * _sigmoid(v0 + lora2[:, D_RWKV:])
    else:
        vout_ref[0] = v
    ops_ref[0] = r
    ops_ref[3] = v
    yield 500

    k = shifted(C_K, C_V)
    kk = k * k_k
    sq_hi, sq_lo = _split(kk * kk)
    n2 = _dot(sq_hi, ones_ref[...]) + _dot(sq_lo, ones_ref[...])
    yield 400
    kk = kk / jnp.maximum(jnp.sqrt(n2), 1e-12)
    ops_ref[4] = kk
    ops_ref[5] = a * kk
    k = k * (1.0 + (a - 1.0) * k_a)
    ops_ref[2] = k
    bonus_ref[0] = _dot((r * k * r_k).astype(BF16), ones_ref[...]) * v
    yield 600

    yield from _wkv_stages(ops_ref, y_ref, h_ref)


def _back_stages(final, in_refs, out_refs, scr_refs):
    (y_ref, g_ref, bonus_ref, x_ref, mod_ref, vec_ref, g1_ref, g2_ref, wgp_ref, avg_ref, poolw_ref, pa_ref, pb_ref,
     wout_ref, wgu_ref, wd_ref) = in_refs[:16]
    gf_ref = in_refs[16] if final else None
    (o_ref,) = out_refs
    halo_ref, act_ref = scr_refs
    s = pl.program_id(1)
    ts = x_ref.shape[1]
    d = x_ref.shape[2]
    n_pool = halo_ref.shape[1]
    lnx_g, lnx_b, pool_scale = vec_ref[0:1, :], vec_ref[1:2, :], vec_ref[2:3, :]
    mod = mod_ref[0]
    x = x_ref[0]
    h1 = _rms_mod(x, g1_ref[...], mod[:, d:2 * d], mod[:, 0:d]).astype(BF16)

    @pl.when(s == 0)
    def _():
        halo_ref[0:POOL_HALO, :] = jnp.zeros((POOL_HALO, halo_ref.shape[1]), F32)

    @pl.when(s > 0)
    def _():
        halo_ref[0:POOL_HALO, :] = halo_ref[ts:ts + POOL_HALO, :]
    yield 300

    y = y_ref[0]
    z = _dot(h1, wgp_ref[:, 0:n_pool])
    mean = _dot(y.astype(BF16), avg_ref[...])
    yield 450
    gate_a = _dot(h1, wgp_ref[:, n_pool:n_pool + d])
    dev = y - mean
    yield 550
    var = _dot((dev * dev).astype(BF16), avg_ref[...])
    yield 150
    gate_b = _dot(h1, wgp_ref[:, n_pool + d:n_pool + 2 * d])
    halo_ref[POOL_HALO:POOL_HALO + ts, :] = z
    pos = (s * ts + 1 + lax.broadcasted_iota(jnp.int32, (ts, LANES), 0)).astype(F32)
    pooled = []
    for gi, win in enumerate(POOL_WINDOWS):
        acc = halo_ref[:, gi * LANES:(gi + 1) * LANES]
        span = 1
        while span < win:
            acc = acc + pltpu.roll(acc, span, 0)
            span *= 2
        mean_g = acc[POOL_HALO:, :] / jnp.minimum(pos, float(win))
        pooled.append(mean_g - z[:, gi * LANES:(gi + 1) * LANES])
    pooled = jnp.concatenate(pooled, axis=1)
    yield 550
    yb = _dot(pooled.astype(BF16), poolw_ref[...]) * pool_scale
    ya = (dev * lax.rsqrt(var + EPS_GN) * lnx_g + lnx_b + bonus_ref[0]) * g_ref[0]
    yield 200
    proj_a = _dot(ya.astype(BF16), pa_ref[...])
    yield 260
    proj_b = _dot(yb.astype(BF16), pb_ref[...])
    mixed = _sigmoid(gate_a) * proj_a + _sigmoid(gate_b) * proj_b
    yield 300
    x = x + mod[:, 2 * d:3 * d] * _dot(mixed.astype(BF16), wout_ref[...])
    yield 550

    d_ff = wd_ref.shape[0]
    h = _rms_mod(x, g2_ref[...], mod[:, 4 * d:5 * d], mod[:, 3 * d:4 * d]).astype(BF16)
    yield 300
    for j in range(0, d_ff, FF_BLOCK):
        gate = _dot(h, wgu_ref[:, j:j + FF_BLOCK])
        up = _dot(h, wgu_ref[:, d_ff + j:d_ff + j + FF_BLOCK])
        act_ref[:, j:j + FF_BLOCK] = (gate * _sigmoid(gate) * up).astype(BF16)
        yield 520
    act = act_ref[...]
    gt2 = mod[:, 5 * d:6 * d]
    outs = []
    for j in range(0, d, DOWN_BLOCK):
        outs.append(x[:, j:j + DOWN_BLOCK] + gt2[:, j:j + DOWN_BLOCK] * _dot(act, wd_ref[:, j:j + DOWN_BLOCK]))
        yield 710
    out = jnp.concatenate(outs, axis=1)
    if final:
        ms = jnp.mean(out * out, axis=-1, keepdims=True)
        out = out * lax.rsqrt(ms + EPS_RMS) * gf_ref[...]
    o_ref[0] = out
    yield 200


class _Part(NamedTuple):
    name: str
    inputs: Sequence[Any]
    in_specs: Sequence[Any]
    out_shapes: Sequence[Any]
    out_specs: Sequence[Any]
    scratch: Sequence[Any]
    stages: Callable[..., Any]
    cost: float


def _tok_spec(ts, n, row0):
    return pl.BlockSpec((1, ts, n), lambda i, s: (i + row0, s, 0))


def _row_spec(n, row0):
    return pl.BlockSpec((1, 1, n), lambda i, s: (i + row0, 0, 0))


def _const_spec(a):
    return pl.BlockSpec(a.shape, lambda i, s: (0,) * a.ndim, pipeline_mode=pl.Buffered(1))


def _front_part(x, x_row0, mod, mod_row0, rows, g1, w, mu, vecs, up1, up2, ones_bd, vfirst):
    seq, d = x.shape[1], x.shape[2]
    ts = TOKEN_TILE
    has_vres = vfirst is not None
    consts = [g1, w, mu, vecs, up1, up2, ones_bd]
    inputs = [x, mod] + consts
    in_specs = [_tok_spec(ts, d, x_row0), _row_spec(mod.shape[2], mod_row0)] + [_const_spec(a) for a in consts]
    if has_vres:
        inputs.append(vfirst)
        in_specs.append(_tok_spec(ts, D_RWKV, 0))
    n_out = 3 if has_vres else 4
    return _Part(
        name="front",
        inputs=inputs, in_specs=in_specs,
        out_shapes=[jax.ShapeDtypeStruct((rows, seq, D_RWKV), F32)] * n_out,
        out_specs=[_tok_spec(ts, D_RWKV, 0)] * n_out,
        scratch=[pltpu.VMEM((ts + SHIFT_PAD, N_SHIFTED), F32), pltpu.VMEM((6, ts, D_RWKV), F32),
                 pltpu.VMEM((D_RWKV // QUAD, HEAD, QUAD), F32)],
        stages=functools.partial(_front_stages, has_vres),
        cost=11000.0)


def _back_part(y, g, bonus, x, x_row0, mod, mod_row0, rows, vecs, g1, g2, w_gp, avg_bd, pool_bd, proj_a, proj_b,
               w_out, w_gu, w_down, final_g):
    seq, d = x.shape[1], x.shape[2]
    ts = TOKEN_TILE
    final = final_g is not None
    consts = [vecs, g1, g2, w_gp, avg_bd, pool_bd, proj_a, proj_b, w_out, w_gu, w_down]
    if final:
        consts.append(final_g)
    inputs = [y, g, bonus, x, mod] + consts
    in_specs = ([_tok_spec(ts, D_RWKV, 0)] * 3 + [_tok_spec(ts, d, x_row0), _row_spec(mod.shape[2], mod_row0)]
                + [_const_spec(a) for a in consts])
    return _Part(
        name="back_final" if final else "back",
        inputs=inputs, in_specs=in_specs,
        out_shapes=[jax.ShapeDtypeStruct((rows, seq, d), F32)],
        out_specs=[_tok_spec(ts, d, 0)],
        scratch=[pltpu.VMEM((ts + POOL_HALO, pool_bd.shape[0]), F32), pltpu.VMEM((ts, w_down.shape[0]), BF16)],
        stages=functools.partial(_back_stages, final),
        cost=15700.0)


def _parts_kernel(parts, *refs):
    n_in = [len(p.inputs) for p in parts]
    n_out = [len(p.out_shapes) for p in parts]
    n_scr = [len(p.scratch) for p in parts]
    pos = 0
    ins, outs, scrs = [], [], []
    for n in n_in:
        ins.append(refs[pos:pos + n])
        pos += n
    for n in n_out:
        outs.append(refs[pos:pos + n])
        pos += n
    for n in n_scr:
        scrs.append(refs[pos:pos + n])
        pos += n
    gens = [p.stages(ins[i], outs[i], scrs[i]) for i, p in enumerate(parts)]
    done = [0.0] * len(parts)
    live = set(range(len(parts)))
    while live:
        i = min(live, key=lambda j: done[j] / parts[j].cost)
        try:
            done[i] += next(gens[i])
        except StopIteration:
            live.discard(i)


def _run_parts(parts, rows, seq):
    outs = pl.pallas_call(
        functools.partial(_parts_kernel, parts),
        out_shape=[s for p in parts for s in p.out_shapes],
        grid=(rows, seq // TOKEN_TILE),
        in_specs=[s for p in parts for s in p.in_specs],
        out_specs=[s for p in parts for s in p.out_specs],
        scratch_shapes=[s for p in parts for s in p.scratch],
        compiler_params=pltpu.CompilerParams(
            dimension_semantics=("arbitrary", "arbitrary"), vmem_limit_bytes=VMEM_LIMIT),
        name="_".join(p.name for p in parts),
    )(*[a for p in parts for a in p.inputs])
    split, pos = [], 0
    for p in parts:
        split.append(outs[pos:pos + len(p.out_shapes)])
        pos += len(p.out_shapes)
    return split


def _block_diag(blocks):
    n = len(blocks)
    rows = []
    for i, blk in enumerate(blocks):
        rows.append(jnp.concatenate(
            [blk if j == i else jnp.zeros((blk.shape[0], blocks[j].shape[1]), blk.dtype) for j in range(n)], axis=1))
    return jnp.concatenate(rows, axis=0)


def _pack_inproj(w_in_l, v_down_l, mu_shift_l, mu_v_l):
    d = w_in_l.shape[0]
    n_shift = mu_shift_l.shape[0]
    n_vl = N_SHIFTED - n_shift - 64
    vd = jnp.zeros((d, n_vl), F32) if v_down_l is None else v_down_l
    mv = jnp.zeros((n_vl,), F32) if mu_v_l is None else mu_v_l
    w = jnp.concatenate([w_in_l[:, :n_shift], vd, jnp.zeros((d, 64), F32)], axis=1)
    mu = jnp.concatenate([mu_shift_l, mv, jnp.zeros((64,), F32)])
    return w.astype(BF16), w_in_l[:, n_shift:].astype(BF16), mu.reshape(1, N_SHIFTED)


def kernel(x, c, ada_w, ada_b, norm1_g, w_in, v_down, mu_shift, mu_v, w0, w_up, a0, a_up, v0, v_up, g_up, k_k, k_a, r_k, lnx_g, lnx_b, pool_w, pool_scale, proj_a, proj_b, w_out, norm2_g, w_gu, w_down, final_g):
    depth = w_in.shape[0]
    batch, seq, d = x.shape
    half = batch // 2
    mods = _modulation(c, ada_w, ada_b).reshape(depth, batch, 1, 6 * d)
    n_heads = D_RWKV // HEAD
    ones_bd = _block_diag([jnp.ones((HEAD, HEAD), BF16)] * n_heads)
    avg_bd = _block_diag([jnp.full((HEAD, HEAD), 1.0 / HEAD, BF16)] * n_heads)
    zeros_r = jnp.zeros((D_RWKV,), F32)
    n_vl = v_up.shape[1]

    front_w, back_w = [], []
    for l in range(depth):
        w1, w_gp, mu = _pack_inproj(w_in[l], v_down[l - 1] if l else None, mu_shift[l], mu_v[l - 1] if l else None)
        vecs = jnp.stack([w0[l], a0[l], v0[l - 1] if l else zeros_r, k_k[l], k_a[l], r_k[l].reshape(-1),
                          zeros_r, zeros_r])
        up1 = _block_diag([w_up[l], a_up[l]]).astype(BF16)
        up2 = jnp.concatenate([
            jnp.concatenate([g_up[l], jnp.zeros((N_GATE_LORA, D_RWKV), F32)], axis=1),
            jnp.concatenate([jnp.zeros((n_vl, D_RWKV), F32), v_up[l - 1] if l else jnp.zeros((n_vl, D_RWKV), F32)], axis=1),
            jnp.zeros((N_SHIFTED - C_GV - N_GATE_LORA - n_vl, 2 * D_RWKV), F32)], axis=0).astype(BF16)
        g1 = norm1_g[l].reshape(1, d)
        front_w.append((g1, w1, mu, vecs, up1, up2, ones_bd))
        mix_vecs = jnp.stack([lnx_g[l], lnx_b[l], pool_scale[l]] + [zeros_r] * 5)
        pool_bd = _block_diag([pool_w[l, i] for i in range(pool_w.shape[1])]).astype(BF16)
        back_w.append((mix_vecs, g1, norm2_g[l].reshape(1, d), w_gp, avg_bd, pool_bd, proj_a[l].astype(BF16),
                       proj_b[l].astype(BF16), w_out[l].astype(BF16), w_gu[l].astype(BF16), w_down[l].astype(BF16),
                       final_g.reshape(1, d) if l == depth - 1 else None))

    xs = [(x, 0), (x, half)]
    v_first = [None, None]
    fronts = {}

    def front(l, h):
        xa, r0 = xs[h]
        return _front_part(xa, r0, mods[l], h * half, half, *front_w[l], v_first[h])

    def back(l, h):
        y, g, bonus = fronts[(l, h)]
        xa, r0 = xs[h]
        return _back_part(y, g, bonus, xa, r0, mods[l], h * half, half, *back_w[l])

    def finish_front(l, h, outs):
        if l == 0:
            v_first[h] = outs[3]
        fronts[(l, h)] = outs[:3]

    (o,) = _run_parts([front(0, 0)], half, seq)
    finish_front(0, 0, o)
    for l in range(depth):
        o_f, o_b = _run_parts([front(l, 1), back(l, 0)], half, seq)
        finish_front(l, 1, o_f)
        xs[0] = (o_b[0], 0)
        if l + 1 < depth:
            o_f, o_b = _run_parts([front(l + 1, 0), back(l, 1)], half, seq)
            finish_front(l + 1, 0, o_f)
            xs[1] = (o_b[0], 0)
        else:
            (o_b,) = _run_parts([back(l, 1)], half, seq)
            xs[1] = (o_b[0], 0)
    return jnp.concatenate([xs[0][0], xs[1][0]], axis=0)
```

```python
import functools
from typing import Any, Callable, NamedTuple, Sequence

import jax
import jax.numpy as jnp
from jax import lax
from jax.experimental import pallas as pl
from jax.experimental.pallas import tpu as pltpu

F32 = jnp.float32
BF16 = jnp.bfloat16

HEAD = 64
CHUNK = 64
QUAD = 4 * HEAD
LANES = 128
D_RWKV = 512
EPS_RMS = 1e-6
EPS_GN = 64e-5
POOL_WINDOWS = (2, 4, 8, 16)
POOL_HALO = 16
TOKEN_TILE = 256
FF_BLOCK = 256
DOWN_BLOCK = 256
VMEM_LIMIT = 60 * 1024 * 1024

C_R, C_K, C_V = 0, 512, 1024
C_WA = 1536
C_GV = 1664
N_SHIFTED = 1920
N_GATE_LORA = 160
SHIFT_PAD = 8


def _dot(a, b):
    return jnp.dot(a, b, preferred_element_type=F32)


def _split(x):
    hi = x.astype(BF16)
    lo = (x - hi.astype(F32)).astype(BF16)
    return hi, lo


def _sigmoid(x):
    return 1.0 / (1.0 + jnp.exp(-x))


def _diag_dot(a, w_ref):
    n = w_ref.shape[0]
    return jnp.concatenate(
        [_dot(a[:, j:j + QUAD], w_ref[j:j + QUAD, j:j + QUAD]) for j in range(0, n, QUAD)], axis=1)


def _rms_mod(x, g, scale, shift):
    ms = jnp.mean(x * x, axis=-1, keepdims=True)
    return x * lax.rsqrt(ms + EPS_RMS) * g * (1.0 + scale) + shift


def _mod_kernel(c_ref, w_ref, b_ref, o_ref):
    c = c_ref[...]
    c_act = (c * _sigmoid(c)).astype(BF16)
    o_ref[0] = _dot(c_act, w_ref[0].astype(BF16)) + b_ref[0]


def _modulation(c, ada_w, ada_b):
    depth, d, n_mod = ada_w.shape
    batch = c.shape[0]
    nb = 1024
    return pl.pallas_call(
        _mod_kernel,
        out_shape=jax.ShapeDtypeStruct((depth, batch, n_mod), F32),
        grid=(depth, n_mod // nb),
        in_specs=[
            pl.BlockSpec((batch, d), lambda l, j: (0, 0)),
            pl.BlockSpec((1, d, nb), lambda l, j: (l, 0, j)),
            pl.BlockSpec((1, 1, nb), lambda l, j: (l, 0, j)),
        ],
        out_specs=pl.BlockSpec((1, batch, nb), lambda l, j: (l, 0, j)),
        name="adaln_mod",
    )(c, ada_w, ada_b.reshape(depth, 1, n_mod))


def _bd(xq, lane_lo):
    z = jnp.zeros((CHUNK, LANES), xq.dtype)
    zero = jnp.zeros((), xq.dtype)
    x0 = xq[:, :LANES]
    x1 = xq[:, LANES:]
    rows = [
        jnp.concatenate([jnp.where(lane_lo, x0, zero), z], axis=1),
        jnp.concatenate([jnp.where(lane_lo, zero, x0), z], axis=1),
        jnp.concatenate([z, jnp.where(lane_lo, x1, zero)], axis=1),
        jnp.concatenate([z, jnp.where(lane_lo, zero, x1)], axis=1),
    ]
    return jnp.concatenate(rows, axis=0)


def _mm_bd(a, xq, lane_lo):
    return _dot(a.astype(BF16), _bd(xq.astype(BF16), lane_lo))


def _head_transpose(xq):
    halves = []
    for i in range(QUAD // LANES):
        t = xq[:, i * LANES:(i + 1) * LANES].T
        halves.append(jnp.concatenate([t[:HEAD], t[HEAD:]], axis=1))
    return jnp.concatenate(halves, axis=1)


def _wkv_stages(ops_ref, y_ref, h_ref):
    row = lax.broadcasted_iota(jnp.int32, (CHUNK, QUAD), 0)
    col = lax.broadcasted_iota(jnp.int32, (CHUNK, QUAD), 1) % HEAD
    strict = col < row
    incl = col <= row
    eye_q = (col == row).astype(F32)
    lane_lo = lax.broadcasted_iota(jnp.int32, (CHUNK, LANES), 1) < HEAD
    tri2 = (lax.broadcasted_iota(jnp.int32, (CHUNK, 2 * CHUNK), 1) % CHUNK
            <= lax.broadcasted_iota(jnp.int32, (CHUNK, 2 * CHUNK), 0)).astype(BF16)

    n_chunks = ops_ref.shape[1] // CHUNK
    n_quads = ops_ref.shape[2] // QUAD
    state = [h_ref[q] for q in range(n_quads)]

    def chunk_inputs(ci):
        rows = slice(ci * CHUNK, (ci + 1) * CHUNK)
        lw = ops_ref[1, rows, :]
        cum = _dot(tri2, jnp.concatenate(_split(lw), axis=0))
        cum_end = cum[CHUNK - 1:CHUNK, :]
        e_out = jnp.exp(-cum)
        e_tail = jnp.exp(cum_end - cum)
        k_raw = ops_ref[2, rows, :]
        b_raw = ops_ref[5, rows, :]
        return dict(p_end=jnp.exp(cum_end),
                    r=ops_ref[0, rows, :] * jnp.exp(cum),
                    kk=ops_ref[4, rows, :] * jnp.exp(cum - lw),
                    k=k_raw * e_out, b=b_raw * e_out, k_hat=k_raw * e_tail, b_hat=b_raw * e_tail,
                    v=ops_ref[3, rows, :])

    def chain(ci, q, inp):
        sl = slice(q * QUAD, (q + 1) * QUAD)
        rq, kkq, vq = inp["r"][:, sl], inp["kk"][:, sl], inp["v"][:, sl]
        lhs = jnp.concatenate([kkq, rq], axis=0)
        a_k = _mm_bd(lhs, _head_transpose(inp["k"][:, sl]), lane_lo)
        a_b = _mm_bd(lhs, _head_transpose(inp["b"][:, sl]), lane_lo)
        a_kk = jnp.where(strict, a_k[:CHUNK], 0.0)
        a_rk = jnp.where(incl, a_k[CHUNK:], 0.0)
        a_kb = jnp.where(strict, a_b[:CHUNK], 0.0)
        a_rb = jnp.where(incl, a_b[CHUNK:], 0.0)
        bt = _head_transpose(inp["b_hat"][:, sl])
        kt = _head_transpose(inp["k_hat"][:, sl])
        yield
        t_inv = eye_q - a_kb
        pw = _mm_bd(a_kb, a_kb, lane_lo)
        av = _mm_bd(jnp.concatenate([a_kk, a_rk, kt], axis=0), vq, lane_lo)
        akk_v, ark_v, kt_v = av[:CHUNK], av[CHUNK:2 * CHUNK], av[2 * CHUNK:]
        yield
        for _ in range(4):
            both = _mm_bd(jnp.concatenate([pw, t_inv], axis=0), pw, lane_lo)
            pw = both[:CHUNK]
            t_inv = t_inv + both[CHUNK:]
            yield
        t_inv = t_inv + _mm_bd(t_inv, pw, lane_lo)
        yield
        kk_hat = _mm_bd(t_inv, kkq, lane_lo)
        u0 = _mm_bd(t_inv, akk_v, lane_lo)
        yield
        lhs2 = jnp.concatenate([a_rb, bt], axis=0)
        p1 = _mm_bd(lhs2, kk_hat, lane_lo)
        p2 = _mm_bd(lhs2, u0, lane_lo)
        r_hat = rq - p1[:CHUNK]
        y0 = ark_v - p2[:CHUNK]
        m_q = eye_q * inp["p_end"][:, sl] - p1[CHUNK:]
        g_q = kt_v - p2[CHUNK:]
        yield
        m_hi, m_lo = _split(m_q)
        h_hi, h_lo = _split(state[q])
        out = _dot(jnp.concatenate([r_hat.astype(BF16), m_hi, m_lo], axis=0), _bd(h_hi, lane_lo))
        y_ref[0, ci * CHUNK:(ci + 1) * CHUNK, sl] = out[:CHUNK] + y0
        state[q] = out[CHUNK:2 * CHUNK] + (out[2 * CHUNK:] + _dot(m_hi, _bd(h_lo, lane_lo))) + g_q
        yield

    gens = [[chain(ci, q, inp) for q in range(n_quads)] for ci, inp in
            ((ci, chunk_inputs(ci)) for ci in range(n_chunks))]
    yield 600
    n_stages = 10
    for tick in range(n_stages + n_chunks - 1):
        active = 0
        for ci in range(n_chunks):
            if 0 <= tick - ci < n_stages:
                active += 1
                for gen in gens[ci]:
                    next(gen)
        yield 110 * active
    for q in range(n_quads):
        h_ref[q] = state[q]


def _front_stages(has_vres, in_refs, out_refs, scr_refs):
    (x_ref, mod_ref, g1_ref, w_ref, mu_ref, vec_ref, up1_ref, up2_ref, ones_ref) = in_refs[:9]
    vfirst_ref = in_refs[9] if has_vres else None
    if has_vres:
        y_ref, g_ref, bonus_ref = out_refs
        vout_ref = None
    else:
        y_ref, g_ref, bonus_ref, vout_ref = out_refs
    sh_ref, ops_ref, h_ref = scr_refs
    s = pl.program_id(1)
    ts = x_ref.shape[1]
    d = x_ref.shape[2]

    @pl.when(s == 0)
    def _():
        sh_ref[0:SHIFT_PAD, :] = jnp.zeros((SHIFT_PAD, N_SHIFTED), F32)
        h_ref[...] = jnp.zeros_like(h_ref)

    @pl.when(s > 0)
    def _():
        sh_ref[0:SHIFT_PAD, :] = sh_ref[ts:ts + SHIFT_PAD, :]

    mod = mod_ref[0]
    h = _rms_mod(x_ref[0], g1_ref[...], mod[:, d:2 * d], mod[:, 0:d]).astype(BF16)
    yield 300
    for c0 in range(0, N_SHIFTED, 512):
        c1 = min(c0 + 512, N_SHIFTED)
        sh_ref[SHIFT_PAD:SHIFT_PAD + ts, c0:c1] = _dot(h, w_ref[:, c0:c1])
        yield 600

    def shifted(c0, c1):
        cur = sh_ref[SHIFT_PAD:SHIFT_PAD + ts, c0:c1]
        prev = sh_ref[SHIFT_PAD - 1:SHIFT_PAD - 1 + ts, c0:c1]
        return cur + (prev - cur) * mu_ref[:, c0:c1]

    w0, a0, v0 = vec_ref[0:1, :], vec_ref[1:2, :], vec_ref[2:3, :]
    k_k, k_a, r_k = vec_ref[3:4, :], vec_ref[4:5, :], vec_ref[5:6, :]

    wa = shifted(C_WA, C_GV)
    lane = lax.broadcasted_iota(jnp.int32, wa.shape, 1)
    wa = jnp.where(lane < 64, jnp.tanh(wa), wa)
    lora1 = _dot(wa.astype(BF16), up1_ref[...])
    gv = shifted(C_GV, N_SHIFTED)
    lane = lax.broadcasted_iota(jnp.int32, gv.shape, 1)
    gv = jnp.where(lane < N_GATE_LORA, _sigmoid(gv), gv)
    lora2 = _dot(gv.astype(BF16), up2_ref[...])
    yield 400

    xw = w0 + lora1[:, :D_RWKV]
    softplus = jnp.maximum(-xw, 0.0) + jnp.log(1.0 + jnp.exp(-jnp.abs(xw)))
    ops_ref[1] = -jnp.exp(-softplus - 0.5)
    a = _sigmoid(a0 + lora1[:, D_RWKV:])
    g_ref[0] = lora2[:, :D_RWKV]
    yield 500

    r = shifted(C_R, C_K)
    v = shifted(C_V, C_WA)
    if has_vres:
        v = v + (vfirst_ref[0] - v) * _sigmoid(v0 + lora2[:, D_RWKV:])
    else:
        vout_ref[0] = v
    ops_ref[0] = r
    ops_ref[3] = v
    yield 500

    k = shifted(C_K, C_V)
    kk = k * k_k
    sq_hi, sq_lo = _split(kk * kk)
    n2 = _diag_dot(sq_hi, ones_ref) + _diag_dot(sq_lo, ones_ref)
    yield 400
    kk = kk / jnp.maximum(jnp.sqrt(n2), 1e-12)
    ops_ref[4] = kk
    ops_ref[5] = a * kk
    k = k * (1.0 + (a - 1.0) * k_a)
    ops_ref[2] = k
    bonus_ref[0] = _diag_dot((r * k * r_k).astype(BF16), ones_ref) * v
    yield 600

    yield from _wkv_stages(ops_ref, y_ref, h_ref)


def _back_stages(final, in_refs, out_refs, scr_refs):
    (y_ref, g_ref, bonus_ref, x_ref, mod_ref, vec_ref, g1_ref, g2_ref, wgp_ref, avg_ref, poolw_ref, pa_ref, pb_ref,
     wout_ref, wgu_ref, wd_ref) = in_refs[:16]
    gf_ref = in_refs[16] if final else None
    (o_ref,) = out_refs
    halo_ref, act_ref = scr_refs
    s = pl.program_id(1)
    ts = x_ref.shape[1]
    d = x_ref.shape[2]
    n_pool = halo_ref.shape[1]
    lnx_g, lnx_b, pool_scale = vec_ref[0:1, :], vec_ref[1:2, :], vec_ref[2:3, :]
    mod = mod_ref[0]

    @pl.when(s == 0)
    def _():
        halo_ref[0:POOL_HALO, :] = jnp.zeros((POOL_HALO, halo_ref.shape[1]), F32)

    @pl.when(s > 0)
    def _():
        halo_ref[0:POOL_HALO, :] = halo_ref[ts:ts + POOL_HALO, :]

    y = y_ref[0]
    mean = _diag_dot(y.astype(BF16), avg_ref)
    yield 100
    x = x_ref[0]
    h1 = _rms_mod(x, g1_ref[...], mod[:, d:2 * d], mod[:, 0:d]).astype(BF16)
    yield 300
    z = _dot(h1, wgp_ref[:, 0:n_pool])
    dev = y - mean
    yield 300
    gate_a = _dot(h1, wgp_ref[:, n_pool:n_pool + d])
    yield 550
    var = _diag_dot((dev * dev).astype(BF16), avg_ref)
    yield 100
    gate_b = _dot(h1, wgp_ref[:, n_pool + d:n_pool + 2 * d])
    halo_ref[POOL_HALO:POOL_HALO + ts, :] = z
    pos = (s * ts + 1 + lax.broadcasted_iota(jnp.int32, (ts, LANES), 0)).astype(F32)
    pooled = []
    for gi, win in enumerate(POOL_WINDOWS):
        acc = halo_ref[:, gi * LANES:(gi + 1) * LANES]
        span = 1
        while span < win:
            acc = acc + pltpu.roll(acc, span, 0)
            span *= 2
        mean_g = acc[POOL_HALO:, :] / jnp.minimum(pos, float(win))
        pooled.append(mean_g - z[:, gi * LANES:(gi + 1) * LANES])
    pooled = jnp.concatenate(pooled, axis=1)
    yield 550
    yb = _diag_dot(pooled.astype(BF16), poolw_ref) * pool_scale
    ya = (dev * lax.rsqrt(var + EPS_GN) * lnx_g + lnx_b + bonus_ref[0]) * g_ref[0]
    yield 200
    proj_a = _dot(ya.astype(BF16), pa_ref[...])
    yield 260
    proj_b = _dot(yb.astype(BF16), pb_ref[...])
    mixed = _sigmoid(gate_a) * proj_a + _sigmoid(gate_b) * proj_b
    yield 300
    x = x + mod[:, 2 * d:3 * d] * _dot(mixed.astype(BF16), wout_ref[...])
    yield 550

    d_ff = wd_ref.shape[0]
    h = _rms_mod(x, g2_ref[...], mod[:, 4 * d:5 * d], mod[:, 3 * d:4 * d]).astype(BF16)
    yield 300
    for j in range(0, d_ff, FF_BLOCK):
        gate = _dot(h, wgu_ref[:, j:j + FF_BLOCK])
        up = _dot(h, wgu_ref[:, d_ff + j:d_ff + j + FF_BLOCK])
        act_ref[:, j:j + FF_BLOCK] = (gate * _sigmoid(gate) * up).astype(BF16)
        yield 520
    act = act_ref[...]
    gt2 = mod[:, 5 * d:6 * d]
    outs = []
    for j in range(0, d, DOWN_BLOCK):
        outs.append(x[:, j:j + DOWN_BLOCK] + gt2[:, j:j + DOWN_BLOCK] * _dot(act, wd_ref[:, j:j + DOWN_BLOCK]))
        yield 710
    out = jnp.concatenate(outs, axis=1)
    if final:
        ms = jnp.mean(out * out, axis=-1, keepdims=True)
        out = out * lax.rsqrt(ms + EPS_RMS) * gf_ref[...]
    o_ref[0] = out
    yield 200


class _Part(NamedTuple):
    name: str
    inputs: Sequence[Any]
    in_specs: Sequence[Any]
    out_shapes: Sequence[Any]
    out_specs: Sequence[Any]
    scratch: Sequence[Any]
    stages: Callable[..., Any]
    cost: float


def _tok_spec(ts, n, row0):
    return pl.BlockSpec((1, ts, n), lambda i, s: (i + row0, s, 0))


def _row_spec(n, row0):
    return pl.BlockSpec((1, 1, n), lambda i, s: (i + row0, 0, 0))


def _const_spec(a):
    return pl.BlockSpec(a.shape, lambda i, s: (0,) * a.ndim, pipeline_mode=pl.Buffered(1))


def _front_part(x, x_row0, mod, mod_row0, rows, g1, w, mu, vecs, up1, up2, ones_bd, vfirst):
    seq, d = x.shape[1], x.shape[2]
    ts = TOKEN_TILE
    has_vres = vfirst is not None
    consts = [g1, w, mu, vecs, up1, up2, ones_bd]
    inputs = [x, mod] + consts
    in_specs = [_tok_spec(ts, d, x_row0), _row_spec(mod.shape[2], mod_row0)] + [_const_spec(a) for a in consts]
    if has_vres:
        inputs.append(vfirst)
        in_specs.append(_tok_spec(ts, D_RWKV, 0))
    n_out = 3 if has_vres else 4
    return _Part(
        name="front",
        inputs=inputs, in_specs=in_specs,
        out_shapes=[jax.ShapeDtypeStruct((rows, seq, D_RWKV), F32)] * n_out,
        out_specs=[_tok_spec(ts, D_RWKV, 0)] * n_out,
        scratch=[pltpu.VMEM((ts + SHIFT_PAD, N_SHIFTED), F32), pltpu.VMEM((6, ts, D_RWKV), F32),
                 pltpu.VMEM((D_RWKV // QUAD, HEAD, QUAD), F32)],
        stages=functools.partial(_front_stages, has_vres),
        cost=10100.0)


def _back_part(y, g, bonus, x, x_row0, mod, mod_row0, rows, vecs, g1, g2, w_gp, avg_bd, pool_bd, proj_a, proj_b,
               w_out, w_gu, w_down, final_g):
    seq, d = x.shape[1], x.shape[2]
    ts = TOKEN_TILE
    final = final_g is not None
    consts = [vecs, g1, g2, w_gp, avg_bd, pool_bd, proj_a, proj_b, w_out, w_gu, w_down]
    if final:
        consts.append(final_g)
    inputs = [y, g, bonus, x, mod] + consts
    in_specs = ([_tok_spec(ts, D_RWKV, 0)] * 3 + [_tok_spec(ts, d, x_row0), _row_spec(mod.shape[2], mod_row0)]
                + [_const_spec(a) for a in consts])
    return _Part(
        name="back_final" if final else "back",
        inputs=inputs, in_specs=in_specs,
        out_shapes=[jax.ShapeDtypeStruct((rows, seq, d), F32)],
        out_specs=[_tok_spec(ts, d, 0)],
        scratch=[pltpu.VMEM((ts + POOL_HALO, pool_bd.shape[0]), F32), pltpu.VMEM((ts, w_down.shape[0]), BF16)],
        stages=functools.partial(_back_stages, final),
        cost=12300.0)


def _parts_kernel(parts, *refs):
    n_in = [len(p.inputs) for p in parts]
    n_out = [len(p.out_shapes) for p in parts]
    n_scr = [len(p.scratch) for p in parts]
    pos = 0
    ins, outs, scrs = [], [], []
    for n in n_in:
        ins.append(refs[pos:pos + n])
        pos += n
    for n in n_out:
        outs.append(refs[pos:pos + n])
        pos += n
    for n in n_scr:
        scrs.append(refs[pos:pos + n])
        pos += n
    gens = [p.stages(ins[i], outs[i], scrs[i]) for i, p in enumerate(parts)]
    done = [0.0] * len(parts)
    live = set(range(len(parts)))
    while live:
        i = min(live, key=lambda j: done[j] / parts[j].cost)
        try:
            done[i] += next(gens[i])
        except StopIteration:
            live.discard(i)


def _run_parts(parts, rows, seq):
    outs = pl.pallas_call(
        functools.partial(_parts_kernel, parts),
        out_shape=[s for p in parts for s in p.out_shapes],
        grid=(rows, seq // TOKEN_TILE),
        in_specs=[s for p in parts for s in p.in_specs],
        out_specs=[s for p in parts for s in p.out_specs],
        scratch_shapes=[s for p in parts for s in p.scratch],
        compiler_params=pltpu.CompilerParams(
            dimension_semantics=("arbitrary", "arbitrary"), vmem_limit_bytes=VMEM_LIMIT),
        name="_".join(p.name for p in parts),
    )(*[a for p in parts for a in p.inputs])
    split, pos = [], 0
    for p in parts:
        split.append(outs[pos:pos + len(p.out_shapes)])
        pos += len(p.out_shapes)
    return split


def _block_diag(blocks):
    n = len(blocks)
    rows = []
    for i, blk in enumerate(blocks):
        rows.append(jnp.concatenate(
            [blk if j == i else jnp.zeros((blk.shape[0], blocks[j].shape[1]), blk.dtype) for j in range(n)], axis=1))
    return jnp.concatenate(rows, axis=0)


def _pack_inproj(w_in_l, v_down_l, mu_shift_l, mu_v_l):
    d = w_in_l.shape[0]
    n_shift = mu_shift_l.shape[0]
    n_vl = N_SHIFTED - n_shift - 64
    vd = jnp.zeros((d, n_vl), F32) if v_down_l is None else v_down_l
    mv = jnp.zeros((n_vl,), F32) if mu_v_l is None else mu_v_l
    w = jnp.concatenate([w_in_l[:, :n_shift], vd, jnp.zeros((d, 64), F32)], axis=1)
    mu = jnp.concatenate([mu_shift_l, mv, jnp.zeros((64,), F32)])
    return w.astype(BF16), w_in_l[:, n_shift:].astype(BF16), mu.reshape(1, N_SHIFTED)


def kernel(x, c, ada_w, ada_b, norm1_g, w_in, v_down, mu_shift, mu_v, w0, w_up, a0, a_up, v0, v_up, g_up, k_k, k_a, r_k, lnx_g, lnx_b, pool_w, pool_scale, proj_a, proj_b, w_out, norm2_g, w_gu, w_down, final_g):
    depth = w_in.shape[0]
    batch, seq, d = x.shape
    half = batch // 2
    mods = _modulation(c, ada_w, ada_b).reshape(depth, batch, 1, 6 * d)
    n_heads = D_RWKV // HEAD
    ones_bd = _block_diag([jnp.ones((HEAD, HEAD), BF16)] * n_heads)
    avg_bd = _block_diag([jnp.full((HEAD, HEAD), 1.0 / HEAD, BF16)] * n_heads)
    zeros_r = jnp.zeros((D_RWKV,), F32)
    n_vl = v_up.shape[1]

    front_w, back_w = [], []
    for l in range(depth):
        w1, w_gp, mu = _pack_inproj(w_in[l], v_down[l - 1] if l else None, mu_shift[l], mu_v[l - 1] if l else None)
        vecs = jnp.stack([w0[l], a0[l], v0[l - 1] if l else zeros_r, k_k[l], k_a[l], r_k[l].reshape(-1),
                          zeros_r, zeros_r])
        up1 = _block_diag([w_up[l], a_up[l]]).astype(BF16)
        up2 = jnp.concatenate([
            jnp.concatenate([g_up[l], jnp.zeros((N_GATE_LORA, D_RWKV), F32)], axis=1),
            jnp.concatenate([jnp.zeros((n_vl, D_RWKV), F32), v_up[l - 1] if l else jnp.zeros((n_vl, D_RWKV), F32)], axis=1),
            jnp.zeros((N_SHIFTED - C_GV - N_GATE_LORA - n_vl, 2 * D_RWKV), F32)], axis=0).astype(BF16)
        g1 = norm1_g[l].reshape(1, d)
        front_w.append((g1, w1, mu, vecs, up1, up2, ones_bd))
        mix_vecs = jnp.stack([lnx_g[l], lnx_b[l], pool_scale[l]] + [zeros_r] * 5)
        pool_bd = _block_diag([pool_w[l, i] for i in range(pool_w.shape[1])]).astype(BF16)
        back_w.append((mix_vecs, g1, norm2_g[l].reshape(1, d), w_gp, avg_bd, pool_bd, proj_a[l].astype(BF16),
                       proj_b[l].astype(BF16), w_out[l].astype(BF16), w_gu[l].astype(BF16), w_down[l].astype(BF16),
                       final_g.reshape(1, d) if l == depth - 1 else None))

    xs = [(x, 0), (x, half)]
    v_first = [None, None]
    fronts = {}

    def front(l, h):
        xa, r0 = xs[h]
        return _front_part(xa, r0, mods[l], h * half, half, *front_w[l], v_first[h])

    def back(l, h):
        y, g, bonus = fronts[(l, h)]
        xa, r0 = xs[h]
        return _back_part(y, g, bonus, xa, r0, mods[l], h * half, half, *back_w[l])

    def finish_front(l, h, outs):
        if l == 0:
            v_first[h] = outs[3]
        fronts[(l, h)] = outs[:3]

    (o,) = _run_parts([front(0, 0)], half, seq)
    finish_front(0, 0, o)
    for l in range(depth):
        o_f, o_b = _run_parts([front(l, 1), back(l, 0)], half, seq)
        finish_front(l, 1, o_f)
        xs[0] = (o_b[0], 0)
        if l + 1 < depth:
            o_f, o_b = _run_parts([front(l + 1, 0), back(l, 1)], half, seq)
            finish_front(l + 1, 0, o_f)
            xs[1] = (o_b[0], 0)
        else:
            (o_b,) = _run_parts([back(l, 1)], half, seq)
            xs[1] = (o_b[0], 0)
    return jnp.concatenate([xs[0][0], xs[1][0]], axis=0)
```

```python
import functools
from typing import Any, Callable, NamedTuple, Sequence

import jax
import jax.numpy as jnp
from jax import lax
from jax.experimental import pallas as pl
from jax.experimental.pallas import tpu as pltpu

F32 = jnp.float32
BF16 = jnp.bfloat16

HEAD = 64
CHUNK = 64
QUAD = 4 * HEAD
LANES = 128
D_RWKV = 512
EPS_RMS = 1e-6
EPS_GN = 64e-5
POOL_WINDOWS = (2, 4, 8, 16)
POOL_HALO = 16
TOKEN_TILE = 256
FF_BLOCK = 256
DOWN_BLOCK = 256
VMEM_LIMIT = 60 * 1024 * 1024

C_R, C_K, C_V = 0, 512, 1024
C_WA = 1536
C_GV = 1664
N_SHIFTED = 1920
N_GATE_LORA = 160
SHIFT_PAD = 8


def _dot(a, b):
    return jnp.dot(a, b, preferred_element_type=F32)


def _split(x):
    hi = x.astype(BF16)
    lo = (x - hi.astype(F32)).astype(BF16)
    return hi, lo


def _sigmoid(x):
    return 1.0 / (1.0 + jnp.exp(-x))


def _diag_dot(a, w_ref):
    n = w_ref.shape[0]
    return jnp.concatenate(
        [_dot(a[:, j:j + QUAD], w_ref[j:j + QUAD, j:j + QUAD]) for j in range(0, n, QUAD)], axis=1)


def _rms_mod(x, g, scale, shift):
    ms = jnp.mean(x * x, axis=-1, keepdims=True)
    return x * lax.rsqrt(ms + EPS_RMS) * g * (1.0 + scale) + shift


def _mod_kernel(c_ref, w_ref, b_ref, o_ref):
    c = c_ref[...]
    c_act = (c * _sigmoid(c)).astype(BF16)
    o_ref[0] = _dot(c_act, w_ref[0].astype(BF16)) + b_ref[0]


def _modulation(c, ada_w, ada_b):
    depth, d, n_mod = ada_w.shape
    batch = c.shape[0]
    nb = 1024
    return pl.pallas_call(
        _mod_kernel,
        out_shape=jax.ShapeDtypeStruct((depth, batch, n_mod), F32),
        grid=(depth, n_mod // nb),
        in_specs=[
            pl.BlockSpec((batch, d), lambda l, j: (0, 0)),
            pl.BlockSpec((1, d, nb), lambda l, j: (l, 0, j)),
            pl.BlockSpec((1, 1, nb), lambda l, j: (l, 0, j)),
        ],
        out_specs=pl.BlockSpec((1, batch, nb), lambda l, j: (l, 0, j)),
        name="adaln_mod",
    )(c, ada_w, ada_b.reshape(depth, 1, n_mod))


def _bd(xq, lane_lo):
    z = jnp.zeros((CHUNK, LANES), xq.dtype)
    zero = jnp.zeros((), xq.dtype)
    x0 = xq[:, :LANES]
    x1 = xq[:, LANES:]
    rows = [
        jnp.concatenate([jnp.where(lane_lo, x0, zero), z], axis=1),
        jnp.concatenate([jnp.where(lane_lo, zero, x0), z], axis=1),
        jnp.concatenate([z, jnp.where(lane_lo, x1, zero)], axis=1),
        jnp.concatenate([z, jnp.where(lane_lo, zero, x1)], axis=1),
    ]
    return jnp.concatenate(rows, axis=0)


def _mm_bd(a, xq, lane_lo):
    return _dot(a.astype(BF16), _bd(xq.astype(BF16), lane_lo))


def _head_transpose(xq):
    halves = []
    for i in range(QUAD // LANES):
        t = xq[:, i * LANES:(i + 1) * LANES].T
        halves.append(jnp.concatenate([t[:HEAD], t[HEAD:]], axis=1))
    return jnp.concatenate(halves, axis=1)


def _wkv_stages(ops_ref, y_ref, h_ref):
    row = lax.broadcasted_iota(jnp.int32, (CHUNK, QUAD), 0)
    col = lax.broadcasted_iota(jnp.int32, (CHUNK, QUAD), 1) % HEAD
    strict = col < row
    incl = col <= row
    eye_q = (col == row).astype(F32)
    lane_lo = lax.broadcasted_iota(jnp.int32, (CHUNK, LANES), 1) < HEAD
    tri2 = (lax.broadcasted_iota(jnp.int32, (CHUNK, 2 * CHUNK), 1) % CHUNK
            <= lax.broadcasted_iota(jnp.int32, (CHUNK, 2 * CHUNK), 0)).astype(BF16)

    n_chunks = ops_ref.shape[1] // CHUNK
    n_quads = ops_ref.shape[2] // QUAD
    state = [h_ref[q] for q in range(n_quads)]

    def chunk_inputs(ci):
        rows = slice(ci * CHUNK, (ci + 1) * CHUNK)
        lw = ops_ref[1, rows, :]
        cum = _dot(tri2, jnp.concatenate(_split(lw), axis=0))
        cum_end = cum[CHUNK - 1:CHUNK, :]
        e_out = jnp.exp(-cum)
        e_tail = jnp.exp(cum_end - cum)
        k_raw = ops_ref[2, rows, :]
        b_raw = ops_ref[5, rows, :]
        return dict(p_end=jnp.exp(cum_end),
                    r=ops_ref[0, rows, :] * jnp.exp(cum),
                    kk=ops_ref[4, rows, :] * jnp.exp(cum - lw),
                    k=k_raw * e_out, b=b_raw * e_out, k_hat=k_raw * e_tail, b_hat=b_raw * e_tail,
                    v=ops_ref[3, rows, :])

    def chain(ci, q, inp):
        sl = slice(q * QUAD, (q + 1) * QUAD)
        rq, kkq, vq = inp["r"][:, sl], inp["kk"][:, sl], inp["v"][:, sl]
        lhs = jnp.concatenate([kkq, rq], axis=0)
        a_k = _mm_bd(lhs, _head_transpose(inp["k"][:, sl]), lane_lo)
        a_b = _mm_bd(lhs, _head_transpose(inp["b"][:, sl]), lane_lo)
        a_kk = jnp.where(strict, a_k[:CHUNK], 0.0)
        a_rk = jnp.where(incl, a_k[CHUNK:], 0.0)
        a_kb = jnp.where(strict, a_b[:CHUNK], 0.0)
        a_rb = jnp.where(incl, a_b[CHUNK:], 0.0)
        bt = _head_transpose(inp["b_hat"][:, sl])
        kt = _head_transpose(inp["k_hat"][:, sl])
        yield
        t_inv = eye_q - a_kb
        pw = _mm_bd(a_kb, a_kb, lane_lo)
        av = _mm_bd(jnp.concatenate([a_kk, a_rk, kt], axis=0), vq, lane_lo)
        akk_v, ark_v, kt_v = av[:CHUNK], av[CHUNK:2 * CHUNK], av[2 * CHUNK:]
        yield
        for _ in range(4):
            both = _mm_bd(jnp.concatenate([pw, t_inv], axis=0), pw, lane_lo)
            pw = both[:CHUNK]
            t_inv = t_inv + both[CHUNK:]
            yield
        t_inv = t_inv + _mm_bd(t_inv, pw, lane_lo)
        yield
        kk_hat = _mm_bd(t_inv, kkq, lane_lo)
        u0 = _mm_bd(t_inv, akk_v, lane_lo)
        yield
        lhs2 = jnp.concatenate([a_rb, bt], axis=0)
        p1 = _mm_bd(lhs2, kk_hat, lane_lo)
        p2 = _mm_bd(lhs2, u0, lane_lo)
        r_hat = rq - p1[:CHUNK]
        y0 = ark_v - p2[:CHUNK]
        m_q = eye_q * inp["p_end"][:, sl] - p1[CHUNK:]
        g_q = kt_v - p2[CHUNK:]
        yield
        m_hi, m_lo = _split(m_q)
        h_hi, h_lo = _split(state[q])
        out = _dot(jnp.concatenate([r_hat.astype(BF16), m_hi, m_lo], axis=0), _bd(h_hi, lane_lo))
        y_ref[0, ci * CHUNK:(ci + 1) * CHUNK, sl] = out[:CHUNK] + y0
        state[q] = out[CHUNK:2 * CHUNK] + (out[2 * CHUNK:] + _dot(m_hi, _bd(h_lo, lane_lo))) + g_q
        yield

    gens = [[chain(ci, q, inp) for q in range(n_quads)] for ci, inp in
            ((ci, chunk_inputs(ci)) for ci in range(n_chunks))]
    yield 600
    n_stages = 10
    for tick in range(n_stages + n_chunks - 1):
        active = 0
        for ci in range(n_chunks):
            if 0 <= tick - ci < n_stages:
                active += 1
                for gen in gens[ci]:
                    next(gen)
        yield 110 * active
    for q in range(n_quads):
        h_ref[q] = state[q]


def _front_stages(has_vres, in_refs, out_refs, scr_refs):
    x_ref, mod_ref = in_refs[:2]
    g1_ref, w_ref, mu_ref, vec_ref, up1_ref, up2_ref = [r.at[0] for r in in_refs[2:8]]
    ones_ref = in_refs[8]
    vfirst_ref = in_refs[9] if has_vres else None
    if has_vres:
        y_ref, g_ref, bonus_ref = out_refs
        vout_ref = None
    else:
        y_ref, g_ref, bonus_ref, vout_ref = out_refs
    sh_ref, ops_ref, h_ref = scr_refs
    s = pl.program_id(1)
    ts = x_ref.shape[1]
    d = x_ref.shape[2]

    @pl.when(s == 0)
    def _():
        sh_ref[0:SHIFT_PAD, :] = jnp.zeros((SHIFT_PAD, N_SHIFTED), F32)
        h_ref[...] = jnp.zeros_like(h_ref)

    @pl.when(s > 0)
    def _():
        sh_ref[0:SHIFT_PAD, :] = sh_ref[ts:ts + SHIFT_PAD, :]

    mod = mod_ref[0]
    h = _rms_mod(x_ref[0], g1_ref[...], mod[:, d:2 * d], mod[:, 0:d]).astype(BF16)
    yield 300
    for c0 in range(0, N_SHIFTED, 512):
        c1 = min(c0 + 512, N_SHIFTED)
        sh_ref[SHIFT_PAD:SHIFT_PAD + ts, c0:c1] = _dot(h, w_ref[:, c0:c1])
        yield 600

    def shifted(c0, c1):
        cur = sh_ref[SHIFT_PAD:SHIFT_PAD + ts, c0:c1]
        prev = sh_ref[SHIFT_PAD - 1:SHIFT_PAD - 1 + ts, c0:c1]
        return cur + (prev - cur) * mu_ref[:, c0:c1]

    w0, a0, v0 = vec_ref[0:1, :], vec_ref[1:2, :], vec_ref[2:3, :]
    k_k, k_a, r_k = vec_ref[3:4, :], vec_ref[4:5, :], vec_ref[5:6, :]

    wa = shifted(C_WA, C_GV)
    lane = lax.broadcasted_iota(jnp.int32, wa.shape, 1)
    wa = jnp.where(lane < 64, jnp.tanh(wa), wa)
    lora1 = _dot(wa.astype(BF16), up1_ref[...])
    gv = shifted(C_GV, N_SHIFTED)
    lane = lax.broadcasted_iota(jnp.int32, gv.shape, 1)
    gv = jnp.where(lane < N_GATE_LORA, _sigmoid(gv), gv)
    lora2 = _dot(gv.astype(BF16), up2_ref[...])
    yield 400

    xw = w0 + lora1[:, :D_RWKV]
    softplus = jnp.maximum(-xw, 0.0) + jnp.log(1.0 + jnp.exp(-jnp.abs(xw)))
    ops_ref[1] = -jnp.exp(-softplus - 0.5)
    a = _sigmoid(a0 + lora1[:, D_RWKV:])
    g_ref[0] = lora2[:, :D_RWKV]
    yield 500

    r = shifted(C_R, C_K)
    v = shifted(C_V, C_WA)
    if has_vres:
        v = v + (vfirst_ref[0] - v) * _sigmoid(v0 + lora2[:, D_RWKV:])
    else:
        vout_ref[0] = v
    ops_ref[0] = r
    ops_ref[3] = v
    yield 500

    k = shifted(C_K, C_V)
    kk = k * k_k
    sq_hi, sq_lo = _split(kk * kk)
    n2 = _diag_dot(sq_hi, ones_ref) + _diag_dot(sq_lo, ones_ref)
    yield 400
    kk = kk / jnp.maximum(jnp.sqrt(n2), 1e-12)
    ops_ref[4] = kk
    ops_ref[5] = a * kk
    k = k * (1.0 + (a - 1.0) * k_a)
    ops_ref[2] = k
    bonus_ref[0] = _diag_dot((r * k * r_k).astype(BF16), ones_ref) * v
    yield 600

    yield from _wkv_stages(ops_ref, y_ref, h_ref)


def _back_stages(final, in_refs, out_refs, scr_refs):
    y_ref, g_ref, bonus_ref, x_ref, mod_ref = in_refs[:5]
    (vec_ref, g1_ref, g2_ref, wgp_ref, poolw_ref, pa_ref, pb_ref, wout_ref, wgu_ref,
     wd_ref) = [r.at[0] for r in in_refs[5:15]]
    avg_ref = in_refs[15]
    gf_ref = in_refs[16] if final else None
    (o_ref,) = out_refs
    halo_ref, act_ref = scr_refs
    s = pl.program_id(1)
    ts = x_ref.shape[1]
    d = x_ref.shape[2]
    n_pool = halo_ref.shape[1]
    lnx_g, lnx_b, pool_scale = vec_ref[0:1, :], vec_ref[1:2, :], vec_ref[2:3, :]
    mod = mod_ref[0]

    @pl.when(s == 0)
    def _():
        halo_ref[0:POOL_HALO, :] = jnp.zeros((POOL_HALO, halo_ref.shape[1]), F32)

    @pl.when(s > 0)
    def _():
        halo_ref[0:POOL_HALO, :] = halo_ref[ts:ts + POOL_HALO, :]

    y = y_ref[0]
    mean = _diag_dot(y.astype(BF16), avg_ref)
    yield 100
    x = x_ref[0]
    h1 = _rms_mod(x, g1_ref[...], mod[:, d:2 * d], mod[:, 0:d]).astype(BF16)
    yield 300
    z = _dot(h1, wgp_ref[:, 0:n_pool])
    dev = y - mean
    yield 300
    gate_a = _dot(h1, wgp_ref[:, n_pool:n_pool + d])
    yield 550
    var = _diag_dot((dev * dev).astype(BF16), avg_ref)
    yield 100
    gate_b = _dot(h1, wgp_ref[:, n_pool + d:n_pool + 2 * d])
    halo_ref[POOL_HALO:POOL_HALO + ts, :] = z
    pos = (s * ts + 1 + lax.broadcasted_iota(jnp.int32, (ts, LANES), 0)).astype(F32)
    pooled = []
    for gi, win in enumerate(POOL_WINDOWS):
        acc = halo_ref[:, gi * LANES:(gi + 1) * LANES]
        span = 1
        while span < win:
            acc = acc + pltpu.roll(acc, span, 0)
            span *= 2
        mean_g = acc[POOL_HALO:, :] / jnp.minimum(pos, float(win))
        pooled.append(mean_g - z[:, gi * LANES:(gi + 1) * LANES])
    pooled = jnp.concatenate(pooled, axis=1)
    yield 550
    yb = _diag_dot(pooled.astype(BF16), poolw_ref) * pool_scale
    ya = (dev * lax.rsqrt(var + EPS_GN) * lnx_g + lnx_b + bonus_ref[0]) * g_ref[0]
    yield 200
    proj_a = _dot(ya.astype(BF16), pa_ref[...])
    yield 260
    proj_b = _dot(yb.astype(BF16), pb_ref[...])
    mixed = _sigmoid(gate_a) * proj_a + _sigmoid(gate_b) * proj_b
    yield 300
    x = x + mod[:, 2 * d:3 * d] * _dot(mixed.astype(BF16), wout_ref[...])
    yield 550

    d_ff = wd_ref.shape[0]
    h = _rms_mod(x, g2_ref[...], mod[:, 4 * d:5 * d], mod[:, 3 * d:4 * d]).astype(BF16)
    yield 300
    for j in range(0, d_ff, FF_BLOCK):
        gate = _dot(h, wgu_ref[:, j:j + FF_BLOCK])
        up = _dot(h, wgu_ref[:, d_ff + j:d_ff + j + FF_BLOCK])
        act_ref[:, j:j + FF_BLOCK] = (gate * _sigmoid(gate) * up).astype(BF16)
        yield 520
    act = act_ref[...]
    gt2 = mod[:, 5 * d:6 * d]
    outs = []
    for j in range(0, d, DOWN_BLOCK):
        outs.append(x[:, j:j + DOWN_BLOCK] + gt2[:, j:j + DOWN_BLOCK] * _dot(act, wd_ref[:, j:j + DOWN_BLOCK]))
        yield 710
    out = jnp.concatenate(outs, axis=1)
    if final:
        ms = jnp.mean(out * out, axis=-1, keepdims=True)
        out = out * lax.rsqrt(ms + EPS_RMS) * gf_ref[...]
    o_ref[0] = out
    yield 200


class _Part(NamedTuple):
    name: str
    inputs: Sequence[Any]
    in_specs: Sequence[Any]
    out_shapes: Sequence[Any]
    out_specs: Sequence[Any]
    scratch: Sequence[Any]
    stages: Callable[..., Any]
    cost: float


def _tok_spec(ts, n, row0):
    return pl.BlockSpec((1, ts, n), lambda i, s: (i + row0, s, 0))


def _row_spec(n, row0):
    return pl.BlockSpec((1, 1, n), lambda i, s: (i + row0, 0, 0))


def _const_spec(a):
    return pl.BlockSpec(a.shape, lambda i, s: (0,) * a.ndim, pipeline_mode=pl.Buffered(1))


def _layer_spec(a, layer):
    return pl.BlockSpec((1,) + a.shape[1:], lambda i, s: (layer,) + (0,) * (a.ndim - 1),
                        pipeline_mode=pl.Buffered(1))


def _front_part(x, x_row0, mod, mod_row0, rows, layer, stacked, ones_bd, vfirst):
    seq, d = x.shape[1], x.shape[2]
    ts = TOKEN_TILE
    has_vres = vfirst is not None
    inputs = [x, mod] + list(stacked) + [ones_bd]
    in_specs = ([_tok_spec(ts, d, x_row0), _row_spec(mod.shape[2], mod_row0)]
                + [_layer_spec(a, layer) for a in stacked] + [_const_spec(ones_bd)])
    if has_vres:
        inputs.append(vfirst)
        in_specs.append(_tok_spec(ts, D_RWKV, 0))
    n_out = 3 if has_vres else 4
    return _Part(
        name="front",
        inputs=inputs, in_specs=in_specs,
        out_shapes=[jax.ShapeDtypeStruct((rows, seq, D_RWKV), F32)] * n_out,
        out_specs=[_tok_spec(ts, D_RWKV, 0)] * n_out,
        scratch=[pltpu.VMEM((ts + SHIFT_PAD, N_SHIFTED), F32), pltpu.VMEM((6, ts, D_RWKV), F32),
                 pltpu.VMEM((D_RWKV // QUAD, HEAD, QUAD), F32)],
        stages=functools.partial(_front_stages, has_vres),
        cost=10100.0)


def _back_part(y, g, bonus, x, x_row0, mod, mod_row0, rows, layer, stacked, avg_bd, final_g):
    seq, d = x.shape[1], x.shape[2]
    ts = TOKEN_TILE
    final = final_g is not None
    consts = [avg_bd] + ([final_g] if final else [])
    inputs = [y, g, bonus, x, mod] + list(stacked) + consts
    in_specs = ([_tok_spec(ts, D_RWKV, 0)] * 3 + [_tok_spec(ts, d, x_row0), _row_spec(mod.shape[2], mod_row0)]
                + [_layer_spec(a, layer) for a in stacked] + [_const_spec(a) for a in consts])
    pool_bd, w_down = stacked[4], stacked[9]
    return _Part(
        name="back_final" if final else "back",
        inputs=inputs, in_specs=in_specs,
        out_shapes=[jax.ShapeDtypeStruct((rows, seq, d), F32)],
        out_specs=[_tok_spec(ts, d, 0)],
        scratch=[pltpu.VMEM((ts + POOL_HALO, pool_bd.shape[1]), F32), pltpu.VMEM((ts, w_down.shape[1]), BF16)],
        stages=functools.partial(_back_stages, final),
        cost=12300.0)


def _parts_kernel(parts, *refs):
    n_in = [len(p.inputs) for p in parts]
    n_out = [len(p.out_shapes) for p in parts]
    n_scr = [len(p.scratch) for p in parts]
    pos = 0
    ins, outs, scrs = [], [], []
    for n in n_in:
        ins.append(refs[pos:pos + n])
        pos += n
    for n in n_out:
        outs.append(refs[pos:pos + n])
        pos += n
    for n in n_scr:
        scrs.append(refs[pos:pos + n])
        pos += n
    gens = [p.stages(ins[i], outs[i], scrs[i]) for i, p in enumerate(parts)]
    done = [0.0] * len(parts)
    live = set(range(len(parts)))
    while live:
        i = min(live, key=lambda j: done[j] / parts[j].cost)
        try:
            done[i] += next(gens[i])
        except StopIteration:
            live.discard(i)


def _run_parts(parts, rows, seq):
    outs = pl.pallas_call(
        functools.partial(_parts_kernel, parts),
        out_shape=[s for p in parts for s in p.out_shapes],
        grid=(rows, seq // TOKEN_TILE),
        in_specs=[s for p in parts for s in p.in_specs],
        out_specs=[s for p in parts for s in p.out_specs],
        scratch_shapes=[s for p in parts for s in p.scratch],
        compiler_params=pltpu.CompilerParams(
            dimension_semantics=("arbitrary", "arbitrary"), vmem_limit_bytes=VMEM_LIMIT),
        name="_".join(p.name for p in parts),
    )(*[a for p in parts for a in p.inputs])
    split, pos = [], 0
    for p in parts:
        split.append(outs[pos:pos + len(p.out_shapes)])
        pos += len(p.out_shapes)
    return split


def _block_diag(blocks):
    n = len(blocks)
    rows = []
    for i, blk in enumerate(blocks):
        rows.append(jnp.concatenate(
            [blk if j == i else jnp.zeros((blk.shape[0], blocks[j].shape[1]), blk.dtype) for j in range(n)], axis=1))
    return jnp.concatenate(rows, axis=0)


def _first_layer_zero(p):
    return jnp.concatenate([jnp.zeros((1,) + p.shape[1:], p.dtype), p], axis=0)


def _block_diag_stacked(blocks):
    depth = blocks[0].shape[0]
    rows = []
    for i, blk in enumerate(blocks):
        rows.append(jnp.concatenate(
            [blk if j == i else jnp.zeros((depth, blk.shape[1], other.shape[2]), blk.dtype)
             for j, other in enumerate(blocks)], axis=2))
    return jnp.concatenate(rows, axis=1)


def kernel(x, c, ada_w, ada_b, norm1_g, w_in, v_down, mu_shift, mu_v, w0, w_up, a0, a_up, v0, v_up, g_up, k_k, k_a, r_k, lnx_g, lnx_b, pool_w, pool_scale, proj_a, proj_b, w_out, norm2_g, w_gu, w_down, final_g):
    depth = w_in.shape[0]
    batch, seq, d = x.shape
    half = batch // 2
    mods = _modulation(c, ada_w, ada_b).reshape(depth, batch, 1, 6 * d)
    n_heads = D_RWKV // HEAD
    ones_bd = _block_diag([jnp.ones((HEAD, HEAD), BF16)] * n_heads)
    avg_bd = _block_diag([jnp.full((HEAD, HEAD), 1.0 / HEAD, BF16)] * n_heads)

    n_shift = mu_shift.shape[1]
    n_vl = v_up.shape[1]
    n_pad = N_SHIFTED - n_shift - n_vl
    w_rwkv = jnp.concatenate([w_in[:, :, :n_shift], _first_layer_zero(v_down), jnp.zeros((depth, d, n_pad), F32)],
                             axis=2).astype(BF16)
    w_gp = w_in[:, :, n_shift:].astype(BF16)
    mu = jnp.concatenate([mu_shift, _first_layer_zero(mu_v), jnp.zeros((depth, n_pad), F32)],
                         axis=1).reshape(depth, 1, N_SHIFTED)
    zeros_r = jnp.zeros((depth, D_RWKV), F32)
    vecs = jnp.stack([w0, a0, _first_layer_zero(v0), k_k, k_a, r_k.reshape(depth, D_RWKV), zeros_r, zeros_r], axis=1)
    up1 = _block_diag_stacked([w_up, a_up]).astype(BF16)
    up2 = jnp.concatenate([_block_diag_stacked([g_up, _first_layer_zero(v_up)]),
                           jnp.zeros((depth, n_pad, 2 * D_RWKV), F32)], axis=1).astype(BF16)
    g1 = norm1_g.reshape(depth, 1, d)
    front_w = (g1, w_rwkv, mu, vecs, up1, up2)
    mix_vecs = jnp.stack([lnx_g, lnx_b, pool_scale] + [zeros_r] * 5, axis=1)
    pool_bd = _block_diag_stacked([pool_w[:, i] for i in range(pool_w.shape[1])]).astype(BF16)
    back_w = (mix_vecs, g1, norm2_g.reshape(depth, 1, d), w_gp, pool_bd, proj_a.astype(BF16), proj_b.astype(BF16),
              w_out.astype(BF16), w_gu.astype(BF16), w_down.astype(BF16))
    final_g2 = final_g.reshape(1, d)

    xs = [(x, 0), (x, half)]
    v_first = [None, None]
    fronts = {}

    def front(l, h):
        xa, r0 = xs[h]
        return _front_part(xa, r0, mods[l], h * half, half, l, front_w, ones_bd, v_first[h])

    def back(l, h):
        y, g, bonus = fronts[(l, h)]
        xa, r0 = xs[h]
        return _back_part(y, g, bonus, xa, r0, mods[l], h * half, half, l, back_w, avg_bd,
                          final_g2 if l == depth - 1 else None)

    def finish_front(l, h, outs):
        if l == 0:
            v_first[h] = outs[3]
        fronts[(l, h)] = outs[:3]

    (o,) = _run_parts([front(0, 0)], half, seq)
    finish_front(0, 0, o)
    for l in range(depth):
        o_f, o_b = _run_parts([front(l, 1), back(l, 0)], half, seq)
        finish_front(l, 1, o_f)
        xs[0] = (o_b[0], 0)
        if l + 1 < depth:
            o_f, o_b = _run_parts([front(l + 1, 0), back(l, 1)], half, seq)
            finish_front(l + 1, 0, o_f)
            xs[1] = (o_b[0], 0)
        else:
            (o_b,) = _run_parts([back(l, 1)], half, seq)
            xs[1] = (o_b[0], 0)
    return jnp.concatenate([xs[0][0], xs[1][0]], axis=0)
```

```python
import functools
from typing import Any, Callable, NamedTuple, Sequence

import jax
import jax.numpy as jnp
from jax import lax
from jax.experimental import pallas as pl
from jax.experimental.pallas import tpu as pltpu

F32 = jnp.float32
BF16 = jnp.bfloat16

HEAD = 64
CHUNK = 64
QUAD = 4 * HEAD
LANES = 128
D_RWKV = 512
EPS_RMS = 1e-6
EPS_GN = 64e-5
POOL_WINDOWS = (2, 4, 8, 16)
POOL_HALO = 16
TOKEN_TILE = 256
FF_BLOCK = 256
DOWN_BLOCK = 256
VMEM_LIMIT = 60 * 1024 * 1024

C_R, C_K, C_V = 0, 512, 1024
C_WA = 1536
C_GV = 1664
N_SHIFTED = 1920
N_GATE_LORA = 160
SHIFT_PAD = 8


def _dot(a, b):
    return jnp.dot(a, b, preferred_element_type=F32)


def _split(x):
    hi = x.astype(BF16)
    lo = (x - hi.astype(F32)).astype(BF16)
    return hi, lo


def _sigmoid(x):
    return 1.0 / (1.0 + jnp.exp(-x))


def _diag_dot(a, w_ref):
    n = w_ref.shape[0]
    return jnp.concatenate(
        [_dot(a[:, j:j + QUAD], w_ref[j:j + QUAD, j:j + QUAD]) for j in range(0, n, QUAD)], axis=1)


def _rms_mod(x, g, scale, shift):
    ms = jnp.mean(x * x, axis=-1, keepdims=True)
    return x * lax.rsqrt(ms + EPS_RMS) * g * (1.0 + scale) + shift


def _mod_kernel(c_ref, w_ref, b_ref, o_ref):
    c = c_ref[...]
    c_act = (c * _sigmoid(c)).astype(BF16)
    o_ref[0] = _dot(c_act, w_ref[0].astype(BF16)) + b_ref[0]


def _modulation(c, ada_w, ada_b):
    depth, d, n_mod = ada_w.shape
    batch = c.shape[0]
    nb = 1024
    return pl.pallas_call(
        _mod_kernel,
        out_shape=jax.ShapeDtypeStruct((depth, batch, n_mod), F32),
        grid=(depth, n_mod // nb),
        in_specs=[
            pl.BlockSpec((batch, d), lambda l, j: (0, 0)),
            pl.BlockSpec((1, d, nb), lambda l, j: (l, 0, j)),
            pl.BlockSpec((1, 1, nb), lambda l, j: (l, 0, j)),
        ],
        out_specs=pl.BlockSpec((1, batch, nb), lambda l, j: (l, 0, j)),
        name="adaln_mod",
    )(c, ada_w, ada_b.reshape(depth, 1, n_mod))


def _bd(xq, lane_lo):
    z = jnp.zeros((CHUNK, LANES), xq.dtype)
    zero = jnp.zeros((), xq.dtype)
    x0 = xq[:, :LANES]
    x1 = xq[:, LANES:]
    rows = [
        jnp.concatenate([jnp.where(lane_lo, x0, zero), z], axis=1),
        jnp.concatenate([jnp.where(lane_lo, zero, x0), z], axis=1),
        jnp.concatenate([z, jnp.where(lane_lo, x1, zero)], axis=1),
        jnp.concatenate([z, jnp.where(lane_lo, zero, x1)], axis=1),
    ]
    return jnp.concatenate(rows, axis=0)


def _mm_bd(a, xq, lane_lo):
    return _dot(a.astype(BF16), _bd(xq.astype(BF16), lane_lo))


def _head_transpose(xq):
    halves = []
    for i in range(QUAD // LANES):
        t = xq[:, i * LANES:(i + 1) * LANES].T
        halves.append(jnp.concatenate([t[:HEAD], t[HEAD:]], axis=1))
    return jnp.concatenate(halves, axis=1)


def _wkv_stages(ops_ref, y_ref, h_ref):
    row = lax.broadcasted_iota(jnp.int32, (CHUNK, QUAD), 0)
    col = lax.broadcasted_iota(jnp.int32, (CHUNK, QUAD), 1) % HEAD
    strict = col < row
    incl = col <= row
    eye_q = (col == row).astype(F32)
    lane_lo = lax.broadcasted_iota(jnp.int32, (CHUNK, LANES), 1) < HEAD
    tri2 = (lax.broadcasted_iota(jnp.int32, (CHUNK, 2 * CHUNK), 1) % CHUNK
            <= lax.broadcasted_iota(jnp.int32, (CHUNK, 2 * CHUNK), 0)).astype(BF16)

    n_chunks = ops_ref.shape[1] // CHUNK
    n_quads = ops_ref.shape[2] // QUAD
    state = [h_ref[q] for q in range(n_quads)]

    def chunk_inputs(ci):
        rows = slice(ci * CHUNK, (ci + 1) * CHUNK)
        lw = ops_ref[1, rows, :]
        cum = _dot(tri2, jnp.concatenate(_split(lw), axis=0))
        cum_end = cum[CHUNK - 1:CHUNK, :]
        e_out = jnp.exp(-cum)
        e_tail = jnp.exp(cum_end - cum)
        k_raw = ops_ref[2, rows, :]
        b_raw = ops_ref[5, rows, :]
        return dict(p_end=jnp.exp(cum_end),
                    r=ops_ref[0, rows, :] * jnp.exp(cum),
                    kk=ops_ref[4, rows, :] * jnp.exp(cum - lw),
                    k=k_raw * e_out, b=b_raw * e_out, k_hat=k_raw * e_tail, b_hat=b_raw * e_tail,
                    v=ops_ref[3, rows, :])

    def chain(ci, q, inp):
        sl = slice(q * QUAD, (q + 1) * QUAD)
        rq, kkq, vq = inp["r"][:, sl], inp["kk"][:, sl], inp["v"][:, sl]
        lhs = jnp.concatenate([kkq, rq], axis=0)
        a_k = _mm_bd(lhs, _head_transpose(inp["k"][:, sl]), lane_lo)
        a_b = _mm_bd(lhs, _head_transpose(inp["b"][:, sl]), lane_lo)
        a_kk = jnp.where(strict, a_k[:CHUNK], 0.0)
        a_rk = jnp.where(incl, a_k[CHUNK:], 0.0)
        a_kb = jnp.where(strict, a_b[:CHUNK], 0.0)
        a_rb = jnp.where(incl, a_b[CHUNK:], 0.0)
        bt = _head_transpose(inp["b_hat"][:, sl])
        kt = _head_transpose(inp["k_hat"][:, sl])
        yield
        t_inv = eye_q - a_kb
        pw = _mm_bd(a_kb, a_kb, lane_lo)
        av = _mm_bd(jnp.concatenate([a_kk, a_rk, kt], axis=0), vq, lane_lo)
        akk_v, ark_v, kt_v = av[:CHUNK], av[CHUNK:2 * CHUNK], av[2 * CHUNK:]
        yield
        for _ in range(4):
            both = _mm_bd(jnp.concatenate([pw, t_inv], axis=0), pw, lane_lo)
            pw = both[:CHUNK]
            t_inv = t_inv + both[CHUNK:]
            yield
        t_inv = t_inv + _mm_bd(t_inv, pw, lane_lo)
        yield
        kk_hat = _mm_bd(t_inv, kkq, lane_lo)
        u0 = _mm_bd(t_inv, akk_v, lane_lo)
        yield
        lhs2 = jnp.concatenate([a_rb, bt], axis=0)
        p1 = _mm_bd(lhs2, kk_hat, lane_lo)
        p2 = _mm_bd(lhs2, u0, lane_lo)
        r_hat = rq - p1[:CHUNK]
        y0 = ark_v - p2[:CHUNK]
        m_q = eye_q * inp["p_end"][:, sl] - p1[CHUNK:]
        g_q = kt_v - p2[CHUNK:]
        yield
        m_hi, m_lo = _split(m_q)
        h_hi, h_lo = _split(state[q])
        out = _dot(jnp.concatenate([r_hat.astype(BF16), m_hi, m_lo], axis=0), _bd(h_hi, lane_lo))
        y_ref[0, ci * CHUNK:(ci + 1) * CHUNK, sl] = out[:CHUNK] + y0
        state[q] = out[CHUNK:2 * CHUNK] + (out[2 * CHUNK:] + _dot(m_hi, _bd(h_lo, lane_lo))) + g_q
        yield

    gens = [[chain(ci, q, inp) for q in range(n_quads)] for ci, inp in
            ((ci, chunk_inputs(ci)) for ci in range(n_chunks))]
    yield 600
    n_stages = 10
    for tick in range(n_stages + n_chunks - 1):
        active = 0
        for ci in range(n_chunks):
            if 0 <= tick - ci < n_stages:
                active += 1
                for gen in gens[ci]:
                    next(gen)
        yield 110 * active
    for q in range(n_quads):
        h_ref[q] = state[q]


def _front_stages(has_vres, in_refs, out_refs, scr_refs):
    x_ref, mod_ref = in_refs[:2]
    g1_ref, w_ref, mu_ref, vec_ref, up1_ref, up2_ref = [r.at[0] for r in in_refs[2:8]]
    ones_ref = in_refs[8]
    vfirst_ref = in_refs[9] if has_vres else None
    if has_vres:
        y_ref, g_ref, bonus_ref = out_refs
        vout_ref = None
    else:
        y_ref, g_ref, bonus_ref, vout_ref = out_refs
    sh_ref, ops_ref, h_ref = scr_refs
    s = pl.program_id(1)
    ts = x_ref.shape[1]
    d = x_ref.shape[2]

    @pl.when(s == 0)
    def _():
        sh_ref[0:SHIFT_PAD, :] = jnp.zeros((SHIFT_PAD, N_SHIFTED), F32)
        h_ref[...] = jnp.zeros_like(h_ref)

    @pl.when(s > 0)
    def _():
        sh_ref[0:SHIFT_PAD, :] = sh_ref[ts:ts + SHIFT_PAD, :]

    mod = mod_ref[0]
    h = _rms_mod(x_ref[0], g1_ref[...], mod[:, d:2 * d], mod[:, 0:d]).astype(BF16)
    yield 300
    for c0 in range(0, N_SHIFTED, 512):
        c1 = min(c0 + 512, N_SHIFTED)
        sh_ref[SHIFT_PAD:SHIFT_PAD + ts, c0:c1] = _dot(h, w_ref[:, c0:c1])
        yield 600

    def shifted(c0, c1):
        both = sh_ref[:, c0:c1]
        cur = both[SHIFT_PAD:, :]
        prev = pltpu.roll(both, 1, 0)[SHIFT_PAD:, :]
        return cur + (prev - cur) * mu_ref[:, c0:c1]

    w0, a0, v0 = vec_ref[0:1, :], vec_ref[1:2, :], vec_ref[2:3, :]
    k_k, k_a, r_k = vec_ref[3:4, :], vec_ref[4:5, :], vec_ref[5:6, :]

    wa = shifted(C_WA, C_GV)
    lane = lax.broadcasted_iota(jnp.int32, wa.shape, 1)
    wa = jnp.where(lane < 64, jnp.tanh(wa), wa)
    lora1 = _dot(wa.astype(BF16), up1_ref[...])
    gv = shifted(C_GV, N_SHIFTED)
    lane = lax.broadcasted_iota(jnp.int32, gv.shape, 1)
    gv = jnp.where(lane < N_GATE_LORA, _sigmoid(gv), gv)
    lora2 = _dot(gv.astype(BF16), up2_ref[...])
    yield 400

    xw = w0 + lora1[:, :D_RWKV]
    softplus = jnp.maximum(-xw, 0.0) + jnp.log(1.0 + jnp.exp(-jnp.abs(xw)))
    ops_ref[1] = -jnp.exp(-softplus - 0.5)
    a = _sigmoid(a0 + lora1[:, D_RWKV:])
    g_ref[0] = lora2[:, :D_RWKV]
    yield 500

    r = shifted(C_R, C_K)
    v = shifted(C_V, C_WA)
    if has_vres:
        v = v + (vfirst_ref[0] - v) * _sigmoid(v0 + lora2[:, D_RWKV:])
    else:
        vout_ref[0] = v
    ops_ref[0] = r
    ops_ref[3] = v
    yield 500

    k = shifted(C_K, C_V)
    kk = k * k_k
    sq_hi, sq_lo = _split(kk * kk)
    n2 = _diag_dot(sq_hi, ones_ref) + _diag_dot(sq_lo, ones_ref)
    yield 400
    kk = kk / jnp.maximum(jnp.sqrt(n2), 1e-12)
    ops_ref[4] = kk
    ops_ref[5] = a * kk
    k = k * (1.0 + (a - 1.0) * k_a)
    ops_ref[2] = k
    bonus_ref[0] = _diag_dot((r * k * r_k).astype(BF16), ones_ref) * v
    yield 600

    yield from _wkv_stages(ops_ref, y_ref, h_ref)


def _back_stages(final, in_refs, out_refs, scr_refs):
    y_ref, g_ref, bonus_ref, x_ref, mod_ref = in_refs[:5]
    (vec_ref, g1_ref, g2_ref, wgp_ref, poolw_ref, pa_ref, pb_ref, wout_ref, wgu_ref,
     wd_ref) = [r.at[0] for r in in_refs[5:15]]
    avg_ref = in_refs[15]
    gf_ref = in_refs[16] if final else None
    (o_ref,) = out_refs
    halo_ref, act_ref = scr_refs
    s = pl.program_id(1)
    ts = x_ref.shape[1]
    d = x_ref.shape[2]
    n_pool = halo_ref.shape[1]
    lnx_g, lnx_b, pool_scale = vec_ref[0:1, :], vec_ref[1:2, :], vec_ref[2:3, :]
    mod = mod_ref[0]

    @pl.when(s == 0)
    def _():
        halo_ref[0:POOL_HALO, :] = jnp.zeros((POOL_HALO, halo_ref.shape[1]), F32)

    @pl.when(s > 0)
    def _():
        halo_ref[0:POOL_HALO, :] = halo_ref[ts:ts + POOL_HALO, :]

    y = y_ref[0]
    mean = _diag_dot(y.astype(BF16), avg_ref)
    yield 100
    x = x_ref[0]
    h1 = _rms_mod(x, g1_ref[...], mod[:, d:2 * d], mod[:, 0:d]).astype(BF16)
    yield 300
    z = _dot(h1, wgp_ref[:, 0:n_pool])
    dev = y - mean
    yield 300
    gate_a = _dot(h1, wgp_ref[:, n_pool:n_pool + d])
    yield 550
    var = _diag_dot((dev * dev).astype(BF16), avg_ref)
    yield 100
    gate_b = _dot(h1, wgp_ref[:, n_pool + d:n_pool + 2 * d])
    halo_ref[POOL_HALO:POOL_HALO + ts, :] = z
    pos = (s * ts + 1 + lax.broadcasted_iota(jnp.int32, (ts, LANES), 0)).astype(F32)
    pooled = []
    for gi, win in enumerate(POOL_WINDOWS):
        acc = halo_ref[:, gi * LANES:(gi + 1) * LANES]
        span = 1
        while span < win:
            acc = acc + pltpu.roll(acc, span, 0)
            span *= 2
        mean_g = acc[POOL_HALO:, :] / jnp.minimum(pos, float(win))
        pooled.append(mean_g - z[:, gi * LANES:(gi + 1) * LANES])
    pooled = jnp.concatenate(pooled, axis=1)
    yield 550
    yb = _diag_dot(pooled.astype(BF16), poolw_ref) * pool_scale
    ya = (dev * lax.rsqrt(var + EPS_GN) * lnx_g + lnx_b + bonus_ref[0]) * g_ref[0]
    yield 200
    proj_a = _dot(ya.astype(BF16), pa_ref[...])
    yield 260
    proj_b = _dot(yb.astype(BF16), pb_ref[...])
    mixed = _sigmoid(gate_a) * proj_a + _sigmoid(gate_b) * proj_b
    yield 300
    x = x + mod[:, 2 * d:3 * d] * _dot(mixed.astype(BF16), wout_ref[...])
    yield 550

    d_ff = wd_ref.shape[0]
    h = _rms_mod(x, g2_ref[...], mod[:, 4 * d:5 * d], mod[:, 3 * d:4 * d]).astype(BF16)
    yield 300
    for j in range(0, d_ff, FF_BLOCK):
        gate = _dot(h, wgu_ref[:, j:j + FF_BLOCK])
        up = _dot(h, wgu_ref[:, d_ff + j:d_ff + j + FF_BLOCK])
        act_ref[:, j:j + FF_BLOCK] = (gate * _sigmoid(gate) * up).astype(BF16)
        yield 520
    act = act_ref[...]
    gt2 = mod[:, 5 * d:6 * d]
    outs = []
    for j in range(0, d, DOWN_BLOCK):
        outs.append(x[:, j:j + DOWN_BLOCK] + gt2[:, j:j + DOWN_BLOCK] * _dot(act, wd_ref[:, j:j + DOWN_BLOCK]))
        yield 710
    out = jnp.concatenate(outs, axis=1)
    if final:
        ms = jnp.mean(out * out, axis=-1, keepdims=True)
        out = out * lax.rsqrt(ms + EPS_RMS) * gf_ref[...]
    o_ref[0] = out
    yield 200


class _Part(NamedTuple):
    name: str
    inputs: Sequence[Any]
    in_specs: Sequence[Any]
    out_shapes: Sequence[Any]
    out_specs: Sequence[Any]
    scratch: Sequence[Any]
    stages: Callable[..., Any]
    cost: float


def _tok_spec(ts, n, row0):
    return pl.BlockSpec((1, ts, n), lambda i, s: (i + row0, s, 0))


def _row_spec(n, row0):
    return pl.BlockSpec((1, 1, n), lambda i, s: (i + row0, 0, 0))


def _const_spec(a):
    return pl.BlockSpec(a.shape, lambda i, s: (0,) * a.ndim, pipeline_mode=pl.Buffered(1))


def _layer_spec(a, layer):
    return pl.BlockSpec((1,) + a.shape[1:], lambda i, s: (layer,) + (0,) * (a.ndim - 1),
                        pipeline_mode=pl.Buffered(1))


def _front_part(x, x_row0, mod, mod_row0, rows, layer, stacked, ones_bd, vfirst):
    seq, d = x.shape[1], x.shape[2]
    ts = TOKEN_TILE
    has_vres = vfirst is not None
    inputs = [x, mod] + list(stacked) + [ones_bd]
    in_specs = ([_tok_spec(ts, d, x_row0), _row_spec(mod.shape[2], mod_row0)]
                + [_layer_spec(a, layer) for a in stacked] + [_const_spec(ones_bd)])
    if has_vres:
        inputs.append(vfirst)
        in_specs.append(_tok_spec(ts, D_RWKV, 0))
    n_out = 3 if has_vres else 4
    return _Part(
        name="front",
        inputs=inputs, in_specs=in_specs,
        out_shapes=[jax.ShapeDtypeStruct((rows, seq, D_RWKV), F32)] * n_out,
        out_specs=[_tok_spec(ts, D_RWKV, 0)] * n_out,
        scratch=[pltpu.VMEM((ts + SHIFT_PAD, N_SHIFTED), F32), pltpu.VMEM((6, ts, D_RWKV), F32),
                 pltpu.VMEM((D_RWKV // QUAD, HEAD, QUAD), F32)],
        stages=functools.partial(_front_stages, has_vres),
        cost=10100.0)


def _back_part(y, g, bonus, x, x_row0, mod, mod_row0, rows, layer, stacked, avg_bd, final_g):
    seq, d = x.shape[1], x.shape[2]
    ts = TOKEN_TILE
    final = final_g is not None
    consts = [avg_bd] + ([final_g] if final else [])
    inputs = [y, g, bonus, x, mod] + list(stacked) + consts
    in_specs = ([_tok_spec(ts, D_RWKV, 0)] * 3 + [_tok_spec(ts, d, x_row0), _row_spec(mod.shape[2], mod_row0)]
                + [_layer_spec(a, layer) for a in stacked] + [_const_spec(a) for a in consts])
    pool_bd, w_down = stacked[4], stacked[9]
    return _Part(
        name="back_final" if final else "back",
        inputs=inputs, in_specs=in_specs,
        out_shapes=[jax.ShapeDtypeStruct((rows, seq, d), F32)],
        out_specs=[_tok_spec(ts, d, 0)],
        scratch=[pltpu.VMEM((ts + POOL_HALO, pool_bd.shape[1]), F32), pltpu.VMEM((ts, w_down.shape[1]), BF16)],
        stages=functools.partial(_back_stages, final),
        cost=12300.0)


def _parts_kernel(parts, *refs):
    n_in = [len(p.inputs) for p in parts]
    n_out = [len(p.out_shapes) for p in parts]
    n_scr = [len(p.scratch) for p in parts]
    pos = 0
    ins, outs, scrs = [], [], []
    for n in n_in:
        ins.append(refs[pos:pos + n])
        pos += n
    for n in n_out:
        outs.append(refs[pos:pos + n])
        pos += n
    for n in n_scr:
        scrs.append(refs[pos:pos + n])
        pos += n
    gens = [p.stages(ins[i], outs[i], scrs[i]) for i, p in enumerate(parts)]
    done = [0.0] * len(parts)
    live = set(range(len(parts)))
    while live:
        i = min(live, key=lambda j: done[j] / parts[j].cost)
        try:
            done[i] += next(gens[i])
        except StopIteration:
            live.discard(i)


def _run_parts(parts, rows, seq):
    outs = pl.pallas_call(
        functools.partial(_parts_kernel, parts),
        out_shape=[s for p in parts for s in p.out_shapes],
        grid=(rows, seq // TOKEN_TILE),
        in_specs=[s for p in parts for s in p.in_specs],
        out_specs=[s for p in parts for s in p.out_specs],
        scratch_shapes=[s for p in parts for s in p.scratch],
        compiler_params=pltpu.CompilerParams(
            dimension_semantics=("arbitrary", "arbitrary"), vmem_limit_bytes=VMEM_LIMIT),
        name="_".join(p.name for p in parts),
    )(*[a for p in parts for a in p.inputs])
    split, pos = [], 0
    for p in parts:
        split.append(outs[pos:pos + len(p.out_shapes)])
        pos += len(p.out_shapes)
    return split


def _block_diag(blocks):
    n = len(blocks)
    rows = []
    for i, blk in enumerate(blocks):
        rows.append(jnp.concatenate(
            [blk if j == i else jnp.zeros((blk.shape[0], blocks[j].shape[1]), blk.dtype) for j in range(n)], axis=1))
    return jnp.concatenate(rows, axis=0)


def _first_layer_zero(p):
    return jnp.concatenate([jnp.zeros((1,) + p.shape[1:], p.dtype), p], axis=0)


def _block_diag_stacked(blocks):
    depth = blocks[0].shape[0]
    rows = []
    for i, blk in enumerate(blocks):
        rows.append(jnp.concatenate(
            [blk if j == i else jnp.zeros((depth, blk.shape[1], other.shape[2]), blk.dtype)
             for j, other in enumerate(blocks)], axis=2))
    return jnp.concatenate(rows, axis=1)


def kernel(x, c, ada_w, ada_b, norm1_g, w_in, v_down, mu_shift, mu_v, w0, w_up, a0, a_up, v0, v_up, g_up, k_k, k_a, r_k, lnx_g, lnx_b, pool_w, pool_scale, proj_a, proj_b, w_out, norm2_g, w_gu, w_down, final_g):
    depth = w_in.shape[0]
    batch, seq, d = x.shape
    half = batch // 2
    mods = _modulation(c, ada_w, ada_b).reshape(depth, batch, 1, 6 * d)
    n_heads = D_RWKV // HEAD
    ones_bd = _block_diag([jnp.ones((HEAD, HEAD), BF16)] * n_heads)
    avg_bd = _block_diag([jnp.full((HEAD, HEAD), 1.0 / HEAD, BF16)] * n_heads)

    n_shift = mu_shift.shape[1]
    n_vl = v_up.shape[1]
    n_pad = N_SHIFTED - n_shift - n_vl
    w_in16 = w_in.astype(BF16)
    w_rwkv = jnp.concatenate([w_in16[:, :, :n_shift], _first_layer_zero(v_down).astype(BF16),
                              jnp.zeros((depth, d, n_pad), BF16)], axis=2)
    w_gp = w_in16[:, :, n_shift:]
    mu = jnp.concatenate([mu_shift, _first_layer_zero(mu_v), jnp.zeros((depth, n_pad), F32)],
                         axis=1).reshape(depth, 1, N_SHIFTED)
    zeros_r = jnp.zeros((depth, D_RWKV), F32)
    vecs = jnp.stack([w0, a0, _first_layer_zero(v0), k_k, k_a, r_k.reshape(depth, D_RWKV), zeros_r, zeros_r], axis=1)
    up1 = _block_diag_stacked([w_up, a_up]).astype(BF16)
    up2 = jnp.concatenate([_block_diag_stacked([g_up, _first_layer_zero(v_up)]),
                           jnp.zeros((depth, n_pad, 2 * D_RWKV), F32)], axis=1).astype(BF16)
    g1 = norm1_g.reshape(depth, 1, d)
    front_w = (g1, w_rwkv, mu, vecs, up1, up2)
    mix_vecs = jnp.stack([lnx_g, lnx_b, pool_scale] + [zeros_r] * 5, axis=1)
    pool_bd = _block_diag_stacked([pool_w[:, i] for i in range(pool_w.shape[1])]).astype(BF16)
    back_w = (mix_vecs, g1, norm2_g.reshape(depth, 1, d), w_gp, pool_bd, proj_a.astype(BF16), proj_b.astype(BF16),
              w_out.astype(BF16), w_gu.astype(BF16), w_down.astype(BF16))
    final_g2 = final_g.reshape(1, d)

    xs = [(x, 0), (x, half)]
    v_first = [None, None]
    fronts = {}

    def front(l, h):
        xa, r0 = xs[h]
        return _front_part(xa, r0, mods[l], h * half, half, l, front_w, ones_bd, v_first[h])

    def back(l, h):
        y, g, bonus = fronts[(l, h)]
        xa, r0 = xs[h]
        return _back_part(y, g, bonus, xa, r0, mods[l], h * half, half, l, back_w, avg_bd,
                          final_g2 if l == depth - 1 else None)

    def finish_front(l, h, outs):
        if l == 0:
            v_first[h] = outs[3]
        fronts[(l, h)] = outs[:3]

    (o,) = _run_parts([front(0, 0)], half, seq)
    finish_front(0, 0, o)
    for l in range(depth):
        o_f, o_b = _run_parts([front(l, 1), back(l, 0)], half, seq)
        finish_front(l, 1, o_f)
        xs[0] = (o_b[0], 0)
        if l + 1 < depth:
            o_f, o_b = _run_parts([front(l + 1, 0), back(l, 1)], half, seq)
            finish_front(l + 1, 0, o_f)
        else:
            (o_b,) = _run_parts([back(l, 1)], half, seq)
        xs[1] = (o_b[0], 0)
    return jnp.concatenate([xs[0][0], xs[1][0]], axis=0)
```

```python
import functools
from typing import Any, Callable, NamedTuple, Sequence

import jax
import jax.numpy as jnp
from jax import lax
from jax.experimental import pallas as pl
from jax.experimental.pallas import tpu as pltpu

F32 = jnp.float32
BF16 = jnp.bfloat16

HEAD = 64
CHUNK = 64
QUAD = 4 * HEAD
LANES = 128
D_RWKV = 512
EPS_RMS = 1e-6
EPS_GN = 64e-5
POOL_WINDOWS = (2, 4, 8, 16)
POOL_HALO = 16
TOKEN_TILE = 256
FF_BLOCK = 256
DOWN_BLOCK = 256
VMEM_LIMIT = 60 * 1024 * 1024

C_R, C_K, C_V = 0, 512, 1024
C_WA = 1536
C_GV = 1664
N_SHIFTED = 1920
N_GATE_LORA = 160
SHIFT_PAD = 8


def _dot(a, b):
    return jnp.dot(a, b, preferred_element_type=F32)


def _split(x):
    hi = x.astype(BF16)
    lo = (x - hi.astype(F32)).astype(BF16)
    return hi, lo


def _sigmoid(x):
    return 1.0 / (1.0 + jnp.exp(-x))


def _diag_dot(a, w_ref):
    n = w_ref.shape[0]
    return jnp.concatenate(
        [_dot(a[:, j:j + QUAD], w_ref[j:j + QUAD, j:j + QUAD]) for j in range(0, n, QUAD)], axis=1)


def _rms_mod(x, g, scale, shift):
    ms = jnp.mean(x * x, axis=-1, keepdims=True)
    return x * lax.rsqrt(ms + EPS_RMS) * g * (1.0 + scale) + shift


def _mod_kernel(c_ref, w_ref, b_ref, o_ref):
    c = c_ref[...]
    c_act = (c * _sigmoid(c)).astype(BF16)
    o_ref[0] = _dot(c_act, w_ref[0].astype(BF16)) + b_ref[0]


def _modulation(c, ada_w, ada_b):
    depth, d, n_mod = ada_w.shape
    batch = c.shape[0]
    nb = 1024
    return pl.pallas_call(
        _mod_kernel,
        out_shape=jax.ShapeDtypeStruct((depth, batch, n_mod), F32),
        grid=(depth, n_mod // nb),
        in_specs=[
            pl.BlockSpec((batch, d), lambda l, j: (0, 0)),
            pl.BlockSpec((1, d, nb), lambda l, j: (l, 0, j)),
            pl.BlockSpec((1, 1, nb), lambda l, j: (l, 0, j)),
        ],
        out_specs=pl.BlockSpec((1, batch, nb), lambda l, j: (l, 0, j)),
        name="adaln_mod",
    )(c, ada_w, ada_b.reshape(depth, 1, n_mod))


def _bd(xq, lane_lo):
    z = jnp.zeros((CHUNK, LANES), xq.dtype)
    zero = jnp.zeros((), xq.dtype)
    x0 = xq[:, :LANES]
    x1 = xq[:, LANES:]
    rows = [
        jnp.concatenate([jnp.where(lane_lo, x0, zero), z], axis=1),
        jnp.concatenate([jnp.where(lane_lo, zero, x0), z], axis=1),
        jnp.concatenate([z, jnp.where(lane_lo, x1, zero)], axis=1),
        jnp.concatenate([z, jnp.where(lane_lo, zero, x1)], axis=1),
    ]
    return jnp.concatenate(rows, axis=0)


def _mm_bd(a, xq, lane_lo):
    return _dot(a.astype(BF16), _bd(xq.astype(BF16), lane_lo))


def _head_transpose(xq):
    halves = []
    for i in range(QUAD // LANES):
        t = xq[:, i * LANES:(i + 1) * LANES].T
        halves.append(jnp.concatenate([t[:HEAD], t[HEAD:]], axis=1))
    return jnp.concatenate(halves, axis=1)


def _wkv_stages(ops_ref, y_ref, h_ref):
    row = lax.broadcasted_iota(jnp.int32, (CHUNK, QUAD), 0)
    col = lax.broadcasted_iota(jnp.int32, (CHUNK, QUAD), 1) % HEAD
    strict = col < row
    incl = col <= row
    eye_q = (col == row).astype(F32)
    lane_lo = lax.broadcasted_iota(jnp.int32, (CHUNK, LANES), 1) < HEAD
    tri2 = (lax.broadcasted_iota(jnp.int32, (CHUNK, 2 * CHUNK), 1) % CHUNK
            <= lax.broadcasted_iota(jnp.int32, (CHUNK, 2 * CHUNK), 0)).astype(BF16)

    n_chunks = ops_ref.shape[1] // CHUNK
    n_quads = ops_ref.shape[2] // QUAD
    state = [h_ref[q] for q in range(n_quads)]

    def chunk_inputs(ci):
        rows = slice(ci * CHUNK, (ci + 1) * CHUNK)
        lw = ops_ref[1, rows, :]
        cum = _dot(tri2, jnp.concatenate(_split(lw), axis=0))
        cum_end = cum[CHUNK - 1:CHUNK, :]
        e_out = jnp.exp(-cum)
        e_tail = jnp.exp(cum_end - cum)
        k_raw = ops_ref[2, rows, :]
        b_raw = ops_ref[5, rows, :]
        return dict(p_end=jnp.exp(cum_end),
                    r=ops_ref[0, rows, :] * jnp.exp(cum),
                    kk=ops_ref[4, rows, :] * jnp.exp(cum - lw),
                    k=k_raw * e_out, b=b_raw * e_out, k_hat=k_raw * e_tail, b_hat=b_raw * e_tail,
                    v=ops_ref[3, rows, :])

    def chain(ci, q, inp):
        sl = slice(q * QUAD, (q + 1) * QUAD)
        rq, kkq, vq = inp["r"][:, sl], inp["kk"][:, sl], inp["v"][:, sl]
        lhs = jnp.concatenate([kkq, rq], axis=0)
        a_k = _mm_bd(lhs, _head_transpose(inp["k"][:, sl]), lane_lo)
        a_b = _mm_bd(lhs, _head_transpose(inp["b"][:, sl]), lane_lo)
        a_kk = jnp.where(strict, a_k[:CHUNK], 0.0)
        a_rk = jnp.where(incl, a_k[CHUNK:], 0.0)
        a_kb = jnp.where(strict, a_b[:CHUNK], 0.0)
        a_rb = jnp.where(incl, a_b[CHUNK:], 0.0)
        bt = _head_transpose(inp["b_hat"][:, sl])
        kt = _head_transpose(inp["k_hat"][:, sl])
        yield
        t_inv = eye_q - a_kb
        pw = _mm_bd(a_kb, a_kb, lane_lo)
        av = _mm_bd(jnp.concatenate([a_kk, a_rk, kt], axis=0), vq, lane_lo)
        akk_v, ark_v, kt_v = av[:CHUNK], av[CHUNK:2 * CHUNK], av[2 * CHUNK:]
        yield
        for _ in range(4):
            both = _mm_bd(jnp.concatenate([pw, t_inv], axis=0), pw, lane_lo)
            pw = both[:CHUNK]
            t_inv = t_inv + both[CHUNK:]
            yield
        t_inv = t_inv + _mm_bd(t_inv, pw, lane_lo)
        yield
        kk_hat = _mm_bd(t_inv, kkq, lane_lo)
        u0 = _mm_bd(t_inv, akk_v, lane_lo)
        yield
        lhs2 = jnp.concatenate([a_rb, bt], axis=0)
        p1 = _mm_bd(lhs2, kk_hat, lane_lo)
        p2 = _mm_bd(lhs2, u0, lane_lo)
        r_hat = rq - p1[:CHUNK]
        y0 = ark_v - p2[:CHUNK]
        m_q = eye_q * inp["p_end"][:, sl] - p1[CHUNK:]
        g_q = kt_v - p2[CHUNK:]
        yield
        m_hi, m_lo = _split(m_q)
        h_hi, h_lo = _split(state[q])
        out = _dot(jnp.concatenate([r_hat.astype(BF16), m_hi, m_lo], axis=0), _bd(h_hi, lane_lo))
        y_ref[0, ci * CHUNK:(ci + 1) * CHUNK, sl] = out[:CHUNK] + y0
        state[q] = out[CHUNK:2 * CHUNK] + (out[2 * CHUNK:] + _dot(m_hi, _bd(h_lo, lane_lo))) + g_q
        yield

    gens = [[chain(ci, q, inp) for q in range(n_quads)] for ci, inp in
            ((ci, chunk_inputs(ci)) for ci in range(n_chunks))]
    yield 600
    n_stages = 10
    for tick in range(n_stages + n_chunks - 1):
        active = 0
        for ci in range(n_chunks):
            if 0 <= tick - ci < n_stages:
                active += 1
                for gen in gens[ci]:
                    next(gen)
        yield 110 * active
    for q in range(n_quads):
        h_ref[q] = state[q]


def _front_stages(has_vres, in_refs, out_refs, scr_refs):
    x_ref, mod_ref = in_refs[:2]
    (g1_ref, w_ref, wgv_ref, mu_ref, vec_ref, up1_ref,
     up2_ref) = [r.at[0] for r in in_refs[2:9]]
    ones_ref = in_refs[9]
    vfirst_ref = in_refs[10] if has_vres else None
    if has_vres:
        y_ref, g_ref, bonus_ref = out_refs
        vout_ref = None
    else:
        y_ref, g_ref, bonus_ref, vout_ref = out_refs
    sh_ref, ops_ref, h_ref = scr_refs
    s = pl.program_id(1)
    ts = x_ref.shape[1]
    d = x_ref.shape[2]

    @pl.when(s == 0)
    def _():
        sh_ref[0:SHIFT_PAD, :] = jnp.zeros((SHIFT_PAD, N_SHIFTED), F32)
        h_ref[...] = jnp.zeros_like(h_ref)

    @pl.when(s > 0)
    def _():
        sh_ref[0:SHIFT_PAD, :] = sh_ref[ts:ts + SHIFT_PAD, :]

    mod = mod_ref[0]
    h = _rms_mod(x_ref[0], g1_ref[...], mod[:, d:2 * d], mod[:, 0:d]).astype(BF16)
    yield 300
    for c0 in range(0, C_GV, 512):
        c1 = min(c0 + 512, C_GV)
        sh_ref[SHIFT_PAD:SHIFT_PAD + ts, c0:c1] = _dot(h, w_ref[:, c0:c1])
        yield 600
    sh_ref[SHIFT_PAD:SHIFT_PAD + ts, C_GV:N_SHIFTED] = _dot(h, wgv_ref[...])
    yield 300

    def shifted(c0, c1):
        both = sh_ref[:, c0:c1]
        cur = both[SHIFT_PAD:, :]
        prev = pltpu.roll(both, 1, 0)[SHIFT_PAD:, :]
        return cur + (prev - cur) * mu_ref[:, c0:c1]

    w0, a0, v0 = vec_ref[0:1, :], vec_ref[1:2, :], vec_ref[2:3, :]
    k_k, k_a, r_k = vec_ref[3:4, :], vec_ref[4:5, :], vec_ref[5:6, :]

    wa = shifted(C_WA, C_GV)
    lane = lax.broadcasted_iota(jnp.int32, wa.shape, 1)
    wa = jnp.where(lane < 64, jnp.tanh(wa), wa)
    lora1 = _dot(wa.astype(BF16), up1_ref[...])
    gv = shifted(C_GV, N_SHIFTED)
    lane = lax.broadcasted_iota(jnp.int32, gv.shape, 1)
    gv = jnp.where(lane < N_GATE_LORA, _sigmoid(gv), gv)
    lora2 = _dot(gv.astype(BF16), up2_ref[...])
    yield 400

    xw = w0 + lora1[:, :D_RWKV]
    softplus = jnp.maximum(-xw, 0.0) + jnp.log(1.0 + jnp.exp(-jnp.abs(xw)))
    ops_ref[1] = -jnp.exp(-softplus - 0.5)
    a = _sigmoid(a0 + lora1[:, D_RWKV:])
    g_ref[0] = lora2[:, :D_RWKV]
    yield 500

    r = shifted(C_R, C_K)
    v = shifted(C_V, C_WA)
    if has_vres:
        v = v + (vfirst_ref[0] - v) * _sigmoid(v0 + lora2[:, D_RWKV:])
    else:
        vout_ref[0] = v
    ops_ref[0] = r
    ops_ref[3] = v
    yield 500

    k = shifted(C_K, C_V)
    kk = k * k_k
    sq_hi, sq_lo = _split(kk * kk)
    n2 = _diag_dot(sq_hi, ones_ref) + _diag_dot(sq_lo, ones_ref)
    yield 400
    kk = kk / jnp.maximum(jnp.sqrt(n2), 1e-12)
    ops_ref[4] = kk
    ops_ref[5] = a * kk
    k = k * (1.0 + (a - 1.0) * k_a)
    ops_ref[2] = k
    bonus_ref[0] = _diag_dot((r * k * r_k).astype(BF16), ones_ref) * v
    yield 600

    yield from _wkv_stages(ops_ref, y_ref, h_ref)


def _back_stages(final, merge, in_refs, out_refs, scr_refs):
    y_ref, g_ref, bonus_ref, x_ref, mod_ref = in_refs[:5]
    (vec_ref, g1_ref, g2_ref, wgp_ref, poolw_ref, pa_ref, pb_ref, wout_ref, wgu_ref,
     wd_ref) = [r.at[0] for r in in_refs[5:15]]
    avg_ref = in_refs[15]
    gf_ref = in_refs[16] if final else None
    (o_ref,) = out_refs
    if merge:
        other_ref = in_refs[-1]
        o_ref[0] = other_ref[...]
        o_ref = o_ref.at[1]
    halo_ref, act_ref = scr_refs
    s = pl.program_id(1)
    ts = x_ref.shape[1]
    d = x_ref.shape[2]
    n_pool = halo_ref.shape[1]
    lnx_g, lnx_b, pool_scale = vec_ref[0:1, :], vec_ref[1:2, :], vec_ref[2:3, :]
    mod = mod_ref[0]

    @pl.when(s == 0)
    def _():
        halo_ref[0:POOL_HALO, :] = jnp.zeros((POOL_HALO, halo_ref.shape[1]), F32)

    @pl.when(s > 0)
    def _():
        halo_ref[0:POOL_HALO, :] = halo_ref[ts:ts + POOL_HALO, :]

    y = y_ref[0]
    mean = _diag_dot(y.astype(BF16), avg_ref)
    yield 100
    x = x_ref[0]
    h1 = _rms_mod(x, g1_ref[...], mod[:, d:2 * d], mod[:, 0:d]).astype(BF16)
    yield 300
    z = _dot(h1, wgp_ref[:, 0:n_pool])
    dev = y - mean
    yield 300
    gate_a = _dot(h1, wgp_ref[:, n_pool:n_pool + d])
    yield 550
    var = _diag_dot((dev * dev).astype(BF16), avg_ref)
    yield 100
    gate_b = _dot(h1, wgp_ref[:, n_pool + d:n_pool + 2 * d])
    halo_ref[POOL_HALO:POOL_HALO + ts, :] = z
    pos = (s * ts + 1 + lax.broadcasted_iota(jnp.int32, (ts, LANES), 0)).astype(F32)
    pooled = []
    for gi, win in enumerate(POOL_WINDOWS):
        acc = halo_ref[:, gi * LANES:(gi + 1) * LANES]
        span = 1
        while span < win:
            acc = acc + pltpu.roll(acc, span, 0)
            span *= 2
        mean_g = acc[POOL_HALO:, :] / jnp.minimum(pos, float(win))
        pooled.append(mean_g - z[:, gi * LANES:(gi + 1) * LANES])
    pooled = jnp.concatenate(pooled, axis=1)
    yield 550
    yb = _diag_dot(pooled.astype(BF16), poolw_ref) * pool_scale
    ya = (dev * lax.rsqrt(var + EPS_GN) * lnx_g + lnx_b + bonus_ref[0]) * g_ref[0]
    yield 200
    proj_a = _dot(ya.astype(BF16), pa_ref[...])
    yield 260
    proj_b = _dot(yb.astype(BF16), pb_ref[...])
    mixed = _sigmoid(gate_a) * proj_a + _sigmoid(gate_b) * proj_b
    yield 300
    x = x + mod[:, 2 * d:3 * d] * _dot(mixed.astype(BF16), wout_ref[...])
    yield 550

    d_ff = wd_ref.shape[0]
    h = _rms_mod(x, g2_ref[...], mod[:, 4 * d:5 * d], mod[:, 3 * d:4 * d]).astype(BF16)
    yield 300
    for j in range(0, d_ff, FF_BLOCK):
        w = min(FF_BLOCK, d_ff - j)
        gate = _dot(h, wgu_ref[:, j:j + w])
        up = _dot(h, wgu_ref[:, d_ff + j:d_ff + j + w])
        act_ref[:, j:j + w] = (gate * _sigmoid(gate) * up).astype(BF16)
        yield 2 * w
    act = act_ref[...]
    gt2 = mod[:, 5 * d:6 * d]
    outs = []
    for j in range(0, d, DOWN_BLOCK):
        outs.append(x[:, j:j + DOWN_BLOCK] + gt2[:, j:j + DOWN_BLOCK] * _dot(act, wd_ref[:, j:j + DOWN_BLOCK]))
        yield 2.8 * DOWN_BLOCK
    out = jnp.concatenate(outs, axis=1)
    if final:
        ms = jnp.mean(out * out, axis=-1, keepdims=True)
        out = out * lax.rsqrt(ms + EPS_RMS) * gf_ref[...]
    o_ref[0] = out
    yield 200


class _Part(NamedTuple):
    name: str
    inputs: Sequence[Any]
    in_specs: Sequence[Any]
    out_shapes: Sequence[Any]
    out_specs: Sequence[Any]
    scratch: Sequence[Any]
    stages: Callable[..., Any]
    cost: float


def _tok_spec(ts, n, row0):
    return pl.BlockSpec((1, ts, n), lambda i, s: (i + row0, s, 0))


def _row_spec(n, row0):
    return pl.BlockSpec((1, 1, n), lambda i, s: (i + row0, 0, 0))


def _const_spec(a):
    return pl.BlockSpec(a.shape, lambda i, s: (0,) * a.ndim, pipeline_mode=pl.Buffered(1))


def _layer_spec(a, layer, cols=None):
    shape = (1,) + a.shape[1:-1] + (cols or a.shape[-1],)
    return pl.BlockSpec(shape, lambda i, s: (layer,) + (0,) * (a.ndim - 1), pipeline_mode=pl.Buffered(1))


def _front_part(x, x_row0, mod, mod_row0, rows, layer, stacked, ones_bd, vfirst):
    seq, d = x.shape[1], x.shape[2]
    ts = TOKEN_TILE
    has_vres = vfirst is not None
    inputs = [x, mod] + list(stacked) + [ones_bd]
    in_specs = ([_tok_spec(ts, d, x_row0), _row_spec(mod.shape[2], mod_row0)]
                + [_layer_spec(a, layer, C_GV if j == 1 else None) for j, a in enumerate(stacked)]
                + [_const_spec(ones_bd)])
    if has_vres:
        inputs.append(vfirst)
        in_specs.append(_tok_spec(ts, D_RWKV, 0))
    n_out = 3 if has_vres else 4
    return _Part(
        name="front",
        inputs=inputs, in_specs=in_specs,
        out_shapes=[jax.ShapeDtypeStruct((rows, seq, D_RWKV), F32)] * n_out,
        out_specs=[_tok_spec(ts, D_RWKV, 0)] * n_out,
        scratch=[pltpu.VMEM((ts + SHIFT_PAD, N_SHIFTED), F32), pltpu.VMEM((6, ts, D_RWKV), F32),
                 pltpu.VMEM((D_RWKV // QUAD, HEAD, QUAD), F32)],
        stages=functools.partial(_front_stages, has_vres),
        cost=10100.0)


def _back_part(y, g, bonus, x, x_row0, mod, mod_row0, rows, layer, stacked, avg_bd, final_g, merge_with=None):
    seq, d = x.shape[1], x.shape[2]
    ts = TOKEN_TILE
    final = final_g is not None
    merge = merge_with is not None
    consts = [avg_bd] + ([final_g] if final else [])
    inputs = [y, g, bonus, x, mod] + list(stacked) + consts
    in_specs = ([_tok_spec(ts, D_RWKV, 0)] * 3 + [_tok_spec(ts, d, x_row0), _row_spec(mod.shape[2], mod_row0)]
                + [_layer_spec(a, layer) for a in stacked] + [_const_spec(a) for a in consts])
    out_shape = jax.ShapeDtypeStruct((rows, seq, d), F32)
    out_spec = _tok_spec(ts, d, 0)
    if merge:
        inputs.append(merge_with)
        in_specs.append(_tok_spec(ts, d, 0))
        out_shape = jax.ShapeDtypeStruct((2, rows, seq, d), F32)
        out_spec = pl.BlockSpec((2, 1, ts, d), lambda i, s: (0, i, s, 0))
    pool_bd, w_down = stacked[4], stacked[9]
    return _Part(
        name=("back_final" if final else "back") + ("_merge" if merge else ""),
        inputs=inputs, in_specs=in_specs,
        out_shapes=[out_shape],
        out_specs=[out_spec],
        scratch=[pltpu.VMEM((ts + POOL_HALO, pool_bd.shape[1]), F32), pltpu.VMEM((ts, w_down.shape[1]), BF16)],
        stages=functools.partial(_back_stages, final, merge),
        cost=12300.0)


def _parts_kernel(parts, *refs):
    n_in = [len(p.inputs) for p in parts]
    n_out = [len(p.out_shapes) for p in parts]
    n_scr = [len(p.scratch) for p in parts]
    pos = 0
    ins, outs, scrs = [], [], []
    for n in n_in:
        ins.append(refs[pos:pos + n])
        pos += n
    for n in n_out:
        outs.append(refs[pos:pos + n])
        pos += n
    for n in n_scr:
        scrs.append(refs[pos:pos + n])
        pos += n
    gens = [p.stages(ins[i], outs[i], scrs[i]) for i, p in enumerate(parts)]
    done = [0.0] * len(parts)
    live = set(range(len(parts)))
    while live:
        i = min(live, key=lambda j: done[j] / parts[j].cost)
        try:
            done[i] += next(gens[i])
        except StopIteration:
            live.discard(i)


def _run_parts(parts, rows, seq):
    outs = pl.pallas_call(
        functools.partial(_parts_kernel, parts),
        out_shape=[s for p in parts for s in p.out_shapes],
        grid=(rows, seq // TOKEN_TILE),
        in_specs=[s for p in parts for s in p.in_specs],
        out_specs=[s for p in parts for s in p.out_specs],
        scratch_shapes=[s for p in parts for s in p.scratch],
        compiler_params=pltpu.CompilerParams(
            dimension_semantics=("arbitrary", "arbitrary"), vmem_limit_bytes=VMEM_LIMIT),
        name="_".join(p.name for p in parts),
    )(*[a for p in parts for a in p.inputs])
    split, pos = [], 0
    for p in parts:
        split.append(outs[pos:pos + len(p.out_shapes)])
        pos += len(p.out_shapes)
    return split


def _block_diag(blocks):
    n = len(blocks)
    rows = []
    for i, blk in enumerate(blocks):
        rows.append(jnp.concatenate(
            [blk if j == i else jnp.zeros((blk.shape[0], blocks[j].shape[1]), blk.dtype) for j in range(n)], axis=1))
    return jnp.concatenate(rows, axis=0)


def _first_layer_zero(p):
    return jnp.concatenate([jnp.zeros((1,) + p.shape[1:], p.dtype), p], axis=0)


def _block_diag_stacked(blocks):
    depth = blocks[0].shape[0]
    rows = []
    for i, blk in enumerate(blocks):
        rows.append(jnp.concatenate(
            [blk if j == i else jnp.zeros((depth, blk.shape[1], other.shape[2]), blk.dtype)
             for j, other in enumerate(blocks)], axis=2))
    return jnp.concatenate(rows, axis=1)


def kernel(x, c, ada_w, ada_b, norm1_g, w_in, v_down, mu_shift, mu_v, w0, w_up, a0, a_up, v0, v_up, g_up, k_k, k_a, r_k, lnx_g, lnx_b, pool_w, pool_scale, proj_a, proj_b, w_out, norm2_g, w_gu, w_down, final_g):
    depth = w_in.shape[0]
    batch, seq, d = x.shape
    half = batch // 2
    mods = _modulation(c, ada_w, ada_b).reshape(depth, batch, 1, 6 * d)
    n_heads = D_RWKV // HEAD
    ones_bd = _block_diag([jnp.ones((HEAD, HEAD), BF16)] * n_heads)
    avg_bd = _block_diag([jnp.full((HEAD, HEAD), 1.0 / HEAD, BF16)] * n_heads)

    n_shift = mu_shift.shape[1]
    n_vl = v_up.shape[1]
    n_pad = N_SHIFTED - n_shift - n_vl
    w_in16 = w_in.astype(BF16)
    w_gv = jnp.concatenate([w_in[:, :, C_GV:n_shift], _first_layer_zero(v_down), jnp.zeros((depth, d, n_pad), F32)],
                           axis=2).astype(BF16)
    w_gp = w_in16[:, :, n_shift:]
    mu = jnp.concatenate([mu_shift, _first_layer_zero(mu_v), jnp.zeros((depth, n_pad), F32)],
                         axis=1).reshape(depth, 1, N_SHIFTED)
    zeros_r = jnp.zeros((depth, D_RWKV), F32)
    vecs = jnp.stack([w0, a0, _first_layer_zero(v0), k_k, k_a, r_k.reshape(depth, D_RWKV), zeros_r, zeros_r], axis=1)
    up1 = _block_diag_stacked([w_up, a_up]).astype(BF16)
    up2 = jnp.concatenate([_block_diag_stacked([g_up, _first_layer_zero(v_up)]),
                           jnp.zeros((depth, n_pad, 2 * D_RWKV), F32)], axis=1).astype(BF16)
    g1 = norm1_g.reshape(depth, 1, d)
    front_w = (g1, w_in16, w_gv, mu, vecs, up1, up2)
    mix_vecs = jnp.stack([lnx_g, lnx_b, pool_scale] + [zeros_r] * 5, axis=1)
    pool_bd = _block_diag_stacked([pool_w[:, i] for i in range(pool_w.shape[1])]).astype(BF16)
    back_w = (mix_vecs, g1, norm2_g.reshape(depth, 1, d), w_gp, pool_bd, proj_a.astype(BF16), proj_b.astype(BF16),
              w_out.astype(BF16), w_gu.astype(BF16), w_down.astype(BF16))
    final_g2 = final_g.reshape(1, d)

    xs = [(x, 0), (x, half)]
    v_first = [None, None]
    fronts = {}

    def front(l, h):
        xa, r0 = xs[h]
        return _front_part(xa, r0, mods[l], h * half, half, l, front_w, ones_bd, v_first[h])

    def back(l, h, merge_with=None):
        y, g, bonus = fronts[(l, h)]
        xa, r0 = xs[h]
        return _back_part(y, g, bonus, xa, r0, mods[l], h * half, half, l, back_w, avg_bd,
                          final_g2 if l == depth - 1 else None, merge_with)

    def finish_front(l, h, outs):
        if l == 0:
            v_first[h] = outs[3]
        fronts[(l, h)] = outs[:3]

    (o,) = _run_parts([front(0, 0)], half, seq)
    finish_front(0, 0, o)
    for l in range(depth):
        o_f, o_b = _run_parts([front(l, 1), back(l, 0)], half, seq)
        finish_front(l, 1, o_f)
        xs[0] = (o_b[0], 0)
        if l + 1 < depth:
            o_f, o_b = _run_parts([front(l + 1, 0), back(l, 1)], half, seq)
            finish_front(l + 1, 0, o_f)
            xs[1] = (o_b[0], 0)
    (o_b,) = _run_parts([back(depth - 1, 1, merge_with=xs[0][0])], half, seq)
    return o_b[0].reshape(batch, seq, d)
```

```python
import functools
from typing import Any, Callable, NamedTuple, Sequence

import jax
import jax.numpy as jnp
from jax import lax
from jax.experimental import pallas as pl
from jax.experimental.pallas import tpu as pltpu

F32 = jnp.float32
BF16 = jnp.bfloat16

HEAD = 64
CHUNK = 64
QUAD = 4 * HEAD
LANES = 128
D_RWKV = 512
EPS_RMS = 1e-6
EPS_GN = 64e-5
POOL_WINDOWS = (2, 4, 8, 16)
POOL_HALO = 16
TOKEN_TILE = 256
FF_BLOCK = 256
DOWN_BLOCK = 256
VMEM_LIMIT = 60 * 1024 * 1024

C_R, C_K, C_V = 0, 512, 1024
C_WA = 1536
C_GV = 1664
N_SHIFTED = 1920
N_GATE_LORA = 160
SHIFT_PAD = 8


def _dot(a, b):
    return jnp.dot(a, b, preferred_element_type=F32)


def _split(x):
    hi = x.astype(BF16)
    lo = (x - hi.astype(F32)).astype(BF16)
    return hi, lo


def _sigmoid(x):
    return 1.0 / (1.0 + jnp.exp(-x))


def _diag_dot(a, w_ref):
    n = w_ref.shape[0]
    return jnp.concatenate(
        [_dot(a[:, j:j + QUAD], w_ref[j:j + QUAD, j:j + QUAD]) for j in range(0, n, QUAD)], axis=1)


def _rms_mod(x, g, scale, shift):
    ms = jnp.mean(x * x, axis=-1, keepdims=True)
    return x * lax.rsqrt(ms + EPS_RMS) * g * (1.0 + scale) + shift


def _mod_kernel(c_ref, w_ref, b_ref, o_ref):
    c = c_ref[...]
    c_act = (c * _sigmoid(c)).astype(BF16)
    o_ref[0] = _dot(c_act, w_ref[0].astype(BF16)) + b_ref[0]


def _modulation(c, ada_w, ada_b):
    depth, d, n_mod = ada_w.shape
    batch = c.shape[0]
    nb = 1024
    return pl.pallas_call(
        _mod_kernel,
        out_shape=jax.ShapeDtypeStruct((depth, batch, n_mod), F32),
        grid=(depth, n_mod // nb),
        in_specs=[
            pl.BlockSpec((batch, d), lambda l, j: (0, 0)),
            pl.BlockSpec((1, d, nb), lambda l, j: (l, 0, j)),
            pl.BlockSpec((1, 1, nb), lambda l, j: (l, 0, j)),
        ],
        out_specs=pl.BlockSpec((1, batch, nb), lambda l, j: (l, 0, j)),
        name="adaln_mod",
    )(c, ada_w, ada_b.reshape(depth, 1, n_mod))


def _bd(xq, lane_lo):
    z = jnp.zeros((CHUNK, LANES), xq.dtype)
    zero = jnp.zeros((), xq.dtype)
    x0 = xq[:, :LANES]
    x1 = xq[:, LANES:]
    rows = [
        jnp.concatenate([jnp.where(lane_lo, x0, zero), z], axis=1),
        jnp.concatenate([jnp.where(lane_lo, zero, x0), z], axis=1),
        jnp.concatenate([z, jnp.where(lane_lo, x1, zero)], axis=1),
        jnp.concatenate([z, jnp.where(lane_lo, zero, x1)], axis=1),
    ]
    return jnp.concatenate(rows, axis=0)


def _mm_bd(a, xq, lane_lo):
    return _dot(a.astype(BF16), _bd(xq.astype(BF16), lane_lo))


def _head_transpose(xq):
    halves = []
    for i in range(QUAD // LANES):
        t = xq[:, i * LANES:(i + 1) * LANES].T
        halves.append(jnp.concatenate([t[:HEAD], t[HEAD:]], axis=1))
    return jnp.concatenate(halves, axis=1)


def _wkv_stages(ops_ref, y_ref, h_ref):
    row = lax.broadcasted_iota(jnp.int32, (CHUNK, QUAD), 0)
    col = lax.broadcasted_iota(jnp.int32, (CHUNK, QUAD), 1) % HEAD
    strict = col < row
    incl = col <= row
    eye_q = (col == row).astype(F32)
    lane_lo = lax.broadcasted_iota(jnp.int32, (CHUNK, LANES), 1) < HEAD
    tri2 = (lax.broadcasted_iota(jnp.int32, (CHUNK, 2 * CHUNK), 1) % CHUNK
            <= lax.broadcasted_iota(jnp.int32, (CHUNK, 2 * CHUNK), 0)).astype(BF16)

    n_chunks = ops_ref.shape[1] // CHUNK
    n_quads = ops_ref.shape[2] // QUAD
    state = [h_ref[q] for q in range(n_quads)]

    def chunk_inputs(ci):
        rows = slice(ci * CHUNK, (ci + 1) * CHUNK)
        lw = ops_ref[1, rows, :]
        cum = _dot(tri2, jnp.concatenate(_split(lw), axis=0))
        cum_end = cum[CHUNK - 1:CHUNK, :]
        e_out = jnp.exp(-cum)
        e_tail = jnp.exp(cum_end - cum)
        k_raw = ops_ref[2, rows, :]
        b_raw = ops_ref[5, rows, :]
        return dict(p_end=jnp.exp(cum_end),
                    r=ops_ref[0, rows, :] * jnp.exp(cum),
                    kk=ops_ref[4, rows, :] * jnp.exp(cum - lw),
                    k=k_raw * e_out, b=b_raw * e_out, k_hat=k_raw * e_tail, b_hat=b_raw * e_tail,
                    v=ops_ref[3, rows, :])

    def chain(ci, q, inp):
        sl = slice(q * QUAD, (q + 1) * QUAD)
        rq, kkq, vq = inp["r"][:, sl], inp["kk"][:, sl], inp["v"][:, sl]
        lhs = jnp.concatenate([kkq, rq], axis=0)
        a_k = _mm_bd(lhs, _head_transpose(inp["k"][:, sl]), lane_lo)
        a_b = _mm_bd(lhs, _head_transpose(inp["b"][:, sl]), lane_lo)
        a_kk = jnp.where(strict, a_k[:CHUNK], 0.0)
        a_rk = jnp.where(incl, a_k[CHUNK:], 0.0)
        a_kb = jnp.where(strict, a_b[:CHUNK], 0.0)
        a_rb = jnp.where(incl, a_b[CHUNK:], 0.0)
        bt = _head_transpose(inp["b_hat"][:, sl])
        kt = _head_transpose(inp["k_hat"][:, sl])
        yield
        t_inv = eye_q - a_kb
        pw = _mm_bd(a_kb, a_kb, lane_lo)
        av = _mm_bd(jnp.concatenate([a_kk, a_rk, kt], axis=0), vq, lane_lo)
        akk_v, ark_v, kt_v = av[:CHUNK], av[CHUNK:2 * CHUNK], av[2 * CHUNK:]
        yield
        for _ in range(4):
            both = _mm_bd(jnp.concatenate([pw, t_inv], axis=0), pw, lane_lo)
            pw = both[:CHUNK]
            t_inv = t_inv + both[CHUNK:]
            yield
        t_inv = t_inv + _mm_bd(t_inv, pw, lane_lo)
        yield
        kk_hat = _mm_bd(t_inv, kkq, lane_lo)
        u0 = _mm_bd(t_inv, akk_v, lane_lo)
        yield
        lhs2 = jnp.concatenate([a_rb, bt], axis=0)
        p1 = _mm_bd(lhs2, kk_hat, lane_lo)
        p2 = _mm_bd(lhs2, u0, lane_lo)
        r_hat = rq - p1[:CHUNK]
        y0 = ark_v - p2[:CHUNK]
        m_q = eye_q * inp["p_end"][:, sl] - p1[CHUNK:]
        g_q = kt_v - p2[CHUNK:]
        yield
        m_hi, m_lo = _split(m_q)
        h_hi, h_lo = _split(state[q])
        out = _dot(jnp.concatenate([r_hat.astype(BF16), m_hi, m_lo], axis=0), _bd(h_hi, lane_lo))
        y_ref[0, ci * CHUNK:(ci + 1) * CHUNK, sl] = out[:CHUNK] + y0
        state[q] = out[CHUNK:2 * CHUNK] + (out[2 * CHUNK:] + _dot(m_hi, _bd(h_lo, lane_lo))) + g_q
        yield

    gens = [[chain(ci, q, inp) for q in range(n_quads)] for ci, inp in
            ((ci, chunk_inputs(ci)) for ci in range(n_chunks))]
    yield 600
    n_stages = 10
    for tick in range(n_stages + n_chunks - 1):
        active = 0
        for ci in range(n_chunks):
            if 0 <= tick - ci < n_stages:
                active += 1
                for gen in gens[ci]:
                    next(gen)
        yield 110 * active
    for q in range(n_quads):
        h_ref[q] = state[q]


def _front_stages(has_vres, in_refs, out_refs, scr_refs):
    x_ref, mod_ref = in_refs[:2]
    (g1_ref, w_ref, wgv_ref, mu_ref, vec_ref, up1_ref,
     up2_ref) = [r.at[0] for r in in_refs[2:9]]
    ones_ref = in_refs[9]
    vfirst_ref = in_refs[10] if has_vres else None
    if has_vres:
        y_ref, g_ref, bonus_ref = out_refs
        vout_ref = None
    else:
        y_ref, g_ref, bonus_ref, vout_ref = out_refs
    sh_ref, ops_ref, h_ref = scr_refs
    s = pl.program_id(1)
    ts = x_ref.shape[1]
    d = x_ref.shape[2]

    @pl.when(s == 0)
    def _():
        sh_ref[0:SHIFT_PAD, :] = jnp.zeros((SHIFT_PAD, N_SHIFTED), F32)
        h_ref[...] = jnp.zeros_like(h_ref)

    @pl.when(s > 0)
    def _():
        sh_ref[0:SHIFT_PAD, :] = sh_ref[ts:ts + SHIFT_PAD, :]

    mod = mod_ref[0]
    h = _rms_mod(x_ref[0], g1_ref[...], mod[:, d:2 * d], mod[:, 0:d]).astype(BF16)
    yield 300
    for c0 in range(0, C_GV, 512):
        c1 = min(c0 + 512, C_GV)
        sh_ref[SHIFT_PAD:SHIFT_PAD + ts, c0:c1] = _dot(h, w_ref[:, c0:c1])
        yield 600
    sh_ref[SHIFT_PAD:SHIFT_PAD + ts, C_GV:N_SHIFTED] = _dot(h, wgv_ref[...])
    yield 300

    def shifted(c0, c1):
        both = sh_ref[:, c0:c1]
        cur = both[SHIFT_PAD:, :]
        prev = pltpu.roll(both, 1, 0)[SHIFT_PAD:, :]
        return cur + (prev - cur) * mu_ref[:, c0:c1]

    w0, a0, v0 = vec_ref[0:1, :], vec_ref[1:2, :], vec_ref[2:3, :]
    k_k, k_a, r_k = vec_ref[3:4, :], vec_ref[4:5, :], vec_ref[5:6, :]

    wa = shifted(C_WA, C_GV)
    lane = lax.broadcasted_iota(jnp.int32, wa.shape, 1)
    wa = jnp.where(lane < 64, jnp.tanh(wa), wa)
    lora1 = _dot(wa.astype(BF16), up1_ref[...])
    gv = shifted(C_GV, N_SHIFTED)
    lane = lax.broadcasted_iota(jnp.int32, gv.shape, 1)
    gv = jnp.where(lane < N_GATE_LORA, _sigmoid(gv), gv)
    lora2 = _dot(gv.astype(BF16), up2_ref[...])
    yield 400

    xw = w0 + lora1[:, :D_RWKV]
    softplus = jnp.maximum(-xw, 0.0) + jnp.log(1.0 + jnp.exp(-jnp.abs(xw)))
    ops_ref[1] = -jnp.exp(-softplus - 0.5)
    a = _sigmoid(a0 + lora1[:, D_RWKV:])
    g_ref[0] = lora2[:, :D_RWKV]
    yield 500

    r = shifted(C_R, C_K)
    v = shifted(C_V, C_WA)
    if has_vres:
        v = v + (vfirst_ref[0] - v) * _sigmoid(v0 + lora2[:, D_RWKV:])
    else:
        vout_ref[0] = v
    ops_ref[0] = r
    ops_ref[3] = v
    yield 500

    k = shifted(C_K, C_V)
    kk = k * k_k
    sq_hi, sq_lo = _split(kk * kk)
    n2 = _diag_dot(sq_hi, ones_ref) + _diag_dot(sq_lo, ones_ref)
    yield 400
    kk = kk / jnp.maximum(jnp.sqrt(n2), 1e-12)
    ops_ref[4] = kk
    ops_ref[5] = a * kk
    k = k * (1.0 + (a - 1.0) * k_a)
    ops_ref[2] = k
    bonus_ref[0] = _diag_dot((r * k * r_k).astype(BF16), ones_ref) * v
    yield 600

    yield from _wkv_stages(ops_ref, y_ref, h_ref)


def _back_stages(final, merge, in_refs, out_refs, scr_refs):
    y_ref, g_ref, bonus_ref, x_ref, mod_ref = in_refs[:5]
    (vec_ref, g1_ref, g2_ref, wgp_ref, poolw_ref, pa_ref, pb_ref, wout_ref, wgu_ref,
     wd_ref) = [r.at[0] for r in in_refs[5:15]]
    avg_ref = in_refs[15]
    gf_ref = in_refs[16] if final else None
    (o_ref,) = out_refs
    if merge:
        other_ref = in_refs[-1]
        o_ref[0] = other_ref[...]
        o_ref = o_ref.at[1]
    halo_ref, act_ref = scr_refs
    s = pl.program_id(1)
    ts = x_ref.shape[1]
    d = x_ref.shape[2]
    n_pool = halo_ref.shape[1]
    lnx_g, lnx_b, pool_scale = vec_ref[0:1, :], vec_ref[1:2, :], vec_ref[2:3, :]
    mod = mod_ref[0]

    @pl.when(s == 0)
    def _():
        halo_ref[0:POOL_HALO, :] = jnp.zeros((POOL_HALO, halo_ref.shape[1]), F32)

    @pl.when(s > 0)
    def _():
        halo_ref[0:POOL_HALO, :] = halo_ref[ts:ts + POOL_HALO, :]

    y = y_ref[0]
    mean = _diag_dot(y.astype(BF16), avg_ref)
    yield 100
    x = x_ref[0]
    h1 = _rms_mod(x, g1_ref[...], mod[:, d:2 * d], mod[:, 0:d]).astype(BF16)
    yield 300
    z = _dot(h1, wgp_ref[:, 0:n_pool])
    dev = y - mean
    yield 300
    gate_a = _dot(h1, wgp_ref[:, n_pool:n_pool + d])
    yield 550
    var = _diag_dot((dev * dev).astype(BF16), avg_ref)
    yield 100
    gate_b = _dot(h1, wgp_ref[:, n_pool + d:n_pool + 2 * d])
    halo_ref[POOL_HALO:POOL_HALO + ts, :] = z
    pos = (s * ts + 1 + lax.broadcasted_iota(jnp.int32, (ts, LANES), 0)).astype(F32)
    pooled = []
    for gi, win in enumerate(POOL_WINDOWS):
        acc = halo_ref[:, gi * LANES:(gi + 1) * LANES]
        span = 1
        while span < win:
            acc = acc + pltpu.roll(acc, span, 0)
            span *= 2
        mean_g = acc[POOL_HALO:, :] / jnp.minimum(pos, float(win))
        pooled.append(mean_g - z[:, gi * LANES:(gi + 1) * LANES])
    pooled = jnp.concatenate(pooled, axis=1)
    yield 550
    yb = _diag_dot(pooled.astype(BF16), poolw_ref) * pool_scale
    ya = (dev * lax.rsqrt(var + EPS_GN) * lnx_g + lnx_b + bonus_ref[0]) * g_ref[0]
    yield 200
    proj_a = _dot(ya.astype(BF16), pa_ref[...])
    yield 260
    proj_b = _dot(yb.astype(BF16), pb_ref[...])
    mixed = _sigmoid(gate_a) * proj_a + _sigmoid(gate_b) * proj_b
    yield 300
    x = x + mod[:, 2 * d:3 * d] * _dot(mixed.astype(BF16), wout_ref[...])
    yield 550

    d_ff = wd_ref.shape[0]
    h = _rms_mod(x, g2_ref[...], mod[:, 4 * d:5 * d], mod[:, 3 * d:4 * d]).astype(BF16)
    yield 300
    for j in range(0, d_ff, FF_BLOCK):
        w = min(FF_BLOCK, d_ff - j)
        gate = _dot(h, wgu_ref[:, j:j + w])
        up = _dot(h, wgu_ref[:, d_ff + j:d_ff + j + w])
        act_ref[:, j:j + w] = (gate * _sigmoid(gate) * up).astype(BF16)
        yield 2 * w
    act = act_ref[...]
    gt2 = mod[:, 5 * d:6 * d]
    outs = []
    for j in range(0, d, DOWN_BLOCK):
        outs.append(x[:, j:j + DOWN_BLOCK] + gt2[:, j:j + DOWN_BLOCK] * _dot(act, wd_ref[:, j:j + DOWN_BLOCK]))
        yield 2.8 * DOWN_BLOCK
    out = jnp.concatenate(outs, axis=1)
    if final:
        ms = jnp.mean(out * out, axis=-1, keepdims=True)
        out = out * lax.rsqrt(ms + EPS_RMS) * gf_ref[...]
    o_ref[0] = out
    yield 200


class _Part(NamedTuple):
    name: str
    inputs: Sequence[Any]
    in_specs: Sequence[Any]
    out_shapes: Sequence[Any]
    out_specs: Sequence[Any]
    scratch: Sequence[Any]
    stages: Callable[..., Any]
    cost: float


def _tok_spec(ts, n, row0):
    return pl.BlockSpec((1, ts, n), lambda i, s: (i + row0, s, 0))


def _row_spec(n, row0):
    return pl.BlockSpec((1, 1, n), lambda i, s: (i + row0, 0, 0))


def _const_spec(a):
    return pl.BlockSpec(a.shape, lambda i, s: (0,) * a.ndim, pipeline_mode=pl.Buffered(1))


def _layer_spec(a, layer, cols=None):
    shape = (1,) + a.shape[1:-1] + (cols or a.shape[-1],)
    return pl.BlockSpec(shape, lambda i, s: (layer,) + (0,) * (a.ndim - 1), pipeline_mode=pl.Buffered(1))


def _front_part(x, x_row0, mod, mod_row0, rows, layer, stacked, ones_bd, vfirst):
    seq, d = x.shape[1], x.shape[2]
    ts = TOKEN_TILE
    has_vres = vfirst is not None
    inputs = [x, mod] + list(stacked) + [ones_bd]
    in_specs = ([_tok_spec(ts, d, x_row0), _row_spec(mod.shape[2], mod_row0)]
                + [_layer_spec(a, layer, C_GV if j == 1 else None) for j, a in enumerate(stacked)]
                + [_const_spec(ones_bd)])
    if has_vres:
        inputs.append(vfirst)
        in_specs.append(_tok_spec(ts, D_RWKV, 0))
    n_out = 3 if has_vres else 4
    return _Part(
        name="front",
        inputs=inputs, in_specs=in_specs,
        out_shapes=[jax.ShapeDtypeStruct((rows, seq, D_RWKV), F32)] * n_out,
        out_specs=[_tok_spec(ts, D_RWKV, 0)] * n_out,
        scratch=[pltpu.VMEM((ts + SHIFT_PAD, N_SHIFTED), F32), pltpu.VMEM((6, ts, D_RWKV), F32),
                 pltpu.VMEM((D_RWKV // QUAD, HEAD, QUAD), F32)],
        stages=functools.partial(_front_stages, has_vres),
        cost=10100.0)


def _back_part(y, g, bonus, x, x_row0, mod, mod_row0, rows, layer, stacked, avg_bd, final_g, merge_with=None):
    seq, d = x.shape[1], x.shape[2]
    ts = TOKEN_TILE
    final = final_g is not None
    merge = merge_with is not None
    consts = [avg_bd] + ([final_g] if final else [])
    inputs = [y, g, bonus, x, mod] + list(stacked) + consts
    in_specs = ([_tok_spec(ts, D_RWKV, 0)] * 3 + [_tok_spec(ts, d, x_row0), _row_spec(mod.shape[2], mod_row0)]
                + [_layer_spec(a, layer) for a in stacked] + [_const_spec(a) for a in consts])
    out_shape = jax.ShapeDtypeStruct((rows, seq, d), F32)
    out_spec = _tok_spec(ts, d, 0)
    if merge:
        inputs.append(merge_with)
        in_specs.append(_tok_spec(ts, d, 0))
        out_shape = jax.ShapeDtypeStruct((2, rows, seq, d), F32)
        out_spec = pl.BlockSpec((2, 1, ts, d), lambda i, s: (0, i, s, 0))
    pool_bd, w_down = stacked[4], stacked[9]
    return _Part(
        name=("back_final" if final else "back") + ("_merge" if merge else ""),
        inputs=inputs, in_specs=in_specs,
        out_shapes=[out_shape],
        out_specs=[out_spec],
        scratch=[pltpu.VMEM((ts + POOL_HALO, pool_bd.shape[1]), F32), pltpu.VMEM((ts, w_down.shape[1]), BF16)],
        stages=functools.partial(_back_stages, final, merge),
        cost=12300.0)


def _cast_stages(in_refs, out_refs, scr_refs):
    for a_ref, o_ref in zip(in_refs, out_refs):
        o_ref[...] = a_ref[...].astype(o_ref.dtype)
        yield 1.0


def _cast_part(arrays, grid_shape):
    n_steps = grid_shape[0] * grid_shape[1]
    flat = [a.reshape(-1, a.shape[-1]) for a in arrays]
    for a in flat:
        assert a.shape[0] % (16 * n_steps) == 0, a.shape
    specs = [pl.BlockSpec((a.shape[0] // n_steps, a.shape[1]), lambda i, s: (i * grid_shape[1] + s, 0)) for a in flat]
    return _Part(
        name="cast",
        inputs=flat, in_specs=specs,
        out_shapes=[jax.ShapeDtypeStruct(a.shape, BF16) for a in flat],
        out_specs=specs,
        scratch=[],
        stages=_cast_stages,
        cost=float(len(flat)))


def _parts_kernel(parts, *refs):
    n_in = [len(p.inputs) for p in parts]
    n_out = [len(p.out_shapes) for p in parts]
    n_scr = [len(p.scratch) for p in parts]
    pos = 0
    ins, outs, scrs = [], [], []
    for n in n_in:
        ins.append(refs[pos:pos + n])
        pos += n
    for n in n_out:
        outs.append(refs[pos:pos + n])
        pos += n
    for n in n_scr:
        scrs.append(refs[pos:pos + n])
        pos += n
    gens = [p.stages(ins[i], outs[i], scrs[i]) for i, p in enumerate(parts)]
    done = [0.0] * len(parts)
    live = set(range(len(parts)))
    while live:
        i = min(live, key=lambda j: done[j] / parts[j].cost)
        try:
            done[i] += next(gens[i])
        except StopIteration:
            live.discard(i)


def _run_parts(parts, rows, seq):
    outs = pl.pallas_call(
        functools.partial(_parts_kernel, parts),
        out_shape=[s for p in parts for s in p.out_shapes],
        grid=(rows, seq // TOKEN_TILE),
        in_specs=[s for p in parts for s in p.in_specs],
        out_specs=[s for p in parts for s in p.out_specs],
        scratch_shapes=[s for p in parts for s in p.scratch],
        compiler_params=pltpu.CompilerParams(
            dimension_semantics=("arbitrary", "arbitrary"), vmem_limit_bytes=VMEM_LIMIT),
        name="_".join(p.name for p in parts),
    )(*[a for p in parts for a in p.inputs])
    split, pos = [], 0
    for p in parts:
        split.append(outs[pos:pos + len(p.out_shapes)])
        pos += len(p.out_shapes)
    return split


def _block_diag(blocks):
    n = len(blocks)
    rows = []
    for i, blk in enumerate(blocks):
        rows.append(jnp.concatenate(
            [blk if j == i else jnp.zeros((blk.shape[0], blocks[j].shape[1]), blk.dtype) for j in range(n)], axis=1))
    return jnp.concatenate(rows, axis=0)


def _first_layer_zero(p):
    return jnp.concatenate([jnp.zeros((1,) + p.shape[1:], p.dtype), p], axis=0)


def _block_diag_stacked(blocks):
    depth = blocks[0].shape[0]
    rows = []
    for i, blk in enumerate(blocks):
        rows.append(jnp.concatenate(
            [blk if j == i else jnp.zeros((depth, blk.shape[1], other.shape[2]), blk.dtype)
             for j, other in enumerate(blocks)], axis=2))
    return jnp.concatenate(rows, axis=1)


def kernel(x, c, ada_w, ada_b, norm1_g, w_in, v_down, mu_shift, mu_v, w0, w_up, a0, a_up, v0, v_up, g_up, k_k, k_a, r_k, lnx_g, lnx_b, pool_w, pool_scale, proj_a, proj_b, w_out, norm2_g, w_gu, w_down, final_g):
    depth = w_in.shape[0]
    batch, seq, d = x.shape
    half = batch // 2
    mods = _modulation(c, ada_w, ada_b).reshape(depth, batch, 1, 6 * d)
    n_heads = D_RWKV // HEAD
    ones_bd = _block_diag([jnp.ones((HEAD, HEAD), BF16)] * n_heads)
    avg_bd = _block_diag([jnp.full((HEAD, HEAD), 1.0 / HEAD, BF16)] * n_heads)

    n_shift = mu_shift.shape[1]
    n_vl = v_up.shape[1]
    n_pad = N_SHIFTED - n_shift - n_vl
    w_in16 = w_in.astype(BF16)
    w_gv = jnp.concatenate([w_in[:, :, C_GV:n_shift], _first_layer_zero(v_down), jnp.zeros((depth, d, n_pad), F32)],
                           axis=2).astype(BF16)
    w_gp = w_in16[:, :, n_shift:]
    mu = jnp.concatenate([mu_shift, _first_layer_zero(mu_v), jnp.zeros((depth, n_pad), F32)],
                         axis=1).reshape(depth, 1, N_SHIFTED)
    zeros_r = jnp.zeros((depth, D_RWKV), F32)
    vecs = jnp.stack([w0, a0, _first_layer_zero(v0), k_k, k_a, r_k.reshape(depth, D_RWKV), zeros_r, zeros_r], axis=1)
    up1 = _block_diag_stacked([w_up, a_up]).astype(BF16)
    up2 = jnp.concatenate([_block_diag_stacked([g_up, _first_layer_zero(v_up)]),
                           jnp.zeros((depth, n_pad, 2 * D_RWKV), F32)], axis=1).astype(BF16)
    g1 = norm1_g.reshape(depth, 1, d)
    front_w = (g1, w_in16, w_gv, mu, vecs, up1, up2)
    mix_vecs = jnp.stack([lnx_g, lnx_b, pool_scale] + [zeros_r] * 5, axis=1)
    pool_bd = _block_diag_stacked([pool_w[:, i] for i in range(pool_w.shape[1])]).astype(BF16)
    final_g2 = final_g.reshape(1, d)

    xs = [(x, 0), (x, half)]
    v_first = [None, None]
    fronts = {}

    def front(l, h):
        xa, r0 = xs[h]
        return _front_part(xa, r0, mods[l], h * half, half, l, front_w, ones_bd, v_first[h])

    def back(l, h, merge_with=None):
        y, g, bonus = fronts[(l, h)]
        xa, r0 = xs[h]
        return _back_part(y, g, bonus, xa, r0, mods[l], h * half, half, l, back_w, avg_bd,
                          final_g2 if l == depth - 1 else None, merge_with)

    def finish_front(l, h, outs):
        if l == 0:
            v_first[h] = outs[3]
        fronts[(l, h)] = outs[:3]

    back_f32 = (proj_a, proj_b, w_out, w_gu, w_down)
    o, cast = _run_parts([front(0, 0), _cast_part(back_f32, (half, seq // TOKEN_TILE))], half, seq)
    finish_front(0, 0, o)
    back_w = (mix_vecs, g1, norm2_g.reshape(depth, 1, d), w_gp, pool_bd) + tuple(
        c.reshape(a.shape) for c, a in zip(cast, back_f32))
    for l in range(depth):
        o_f, o_b = _run_parts([front(l, 1), back(l, 0)], half, seq)
        finish_front(l, 1, o_f)
        xs[0] = (o_b[0], 0)
        if l + 1 < depth:
            o_f, o_b = _run_parts([front(l + 1, 0), back(l, 1)], half, seq)
            finish_front(l + 1, 0, o_f)
            xs[1] = (o_b[0], 0)
    (o_b,) = _run_parts([back(depth - 1, 1, merge_with=xs[0][0])], half, seq)
    return o_b[0].reshape(batch, seq, d)
```

```python
import functools
from typing import Any, Callable, NamedTuple, Sequence

import jax
import jax.numpy as jnp
from jax import lax
from jax.experimental import pallas as pl
from jax.experimental.pallas import tpu as pltpu

F32 = jnp.float32
BF16 = jnp.bfloat16

HEAD = 64
CHUNK = 64
QUAD = 4 * HEAD
LANES = 128
D_RWKV = 512
EPS_RMS = 1e-6
EPS_GN = 64e-5
POOL_WINDOWS = (2, 4, 8, 16)
POOL_HALO = 16
TOKEN_TILE = 256
SOLO_TOKEN_TILE = 512
FF_BLOCK = 256
DOWN_BLOCK = 256
VMEM_LIMIT = 60 * 1024 * 1024

C_R, C_K, C_V = 0, 512, 1024
C_WA = 1536
C_GV = 1664
N_SHIFTED = 1920
N_GATE_LORA = 160
SHIFT_PAD = 8


def _dot(a, b):
    return jnp.dot(a, b, preferred_element_type=F32)


def _split(x):
    hi = x.astype(BF16)
    lo = (x - hi.astype(F32)).astype(BF16)
    return hi, lo


def _sigmoid(x):
    return 1.0 / (1.0 + jnp.exp(-x))


def _diag_dot(a, w_ref):
    n = w_ref.shape[0]
    return jnp.concatenate(
        [_dot(a[:, j:j + QUAD], w_ref[j:j + QUAD, j:j + QUAD]) for j in range(0, n, QUAD)], axis=1)


def _rms_mod(x, g, scale, shift):
    ms = jnp.mean(x * x, axis=-1, keepdims=True)
    return x * lax.rsqrt(ms + EPS_RMS) * g * (1.0 + scale) + shift


def _mod_kernel(c_ref, w_ref, b_ref, o_ref):
    c = c_ref[...]
    c_act = (c * _sigmoid(c)).astype(BF16)
    o_ref[0] = _dot(c_act, w_ref[0].astype(BF16)) + b_ref[0]


def _modulation(c, ada_w, ada_b):
    depth, d, n_mod = ada_w.shape
    batch = c.shape[0]
    nb = 1024
    return pl.pallas_call(
        _mod_kernel,
        out_shape=jax.ShapeDtypeStruct((depth, batch, n_mod), F32),
        grid=(depth, n_mod // nb),
        in_specs=[
            pl.BlockSpec((batch, d), lambda l, j: (0, 0)),
            pl.BlockSpec((1, d, nb), lambda l, j: (l, 0, j)),
            pl.BlockSpec((1, 1, nb), lambda l, j: (l, 0, j)),
        ],
        out_specs=pl.BlockSpec((1, batch, nb), lambda l, j: (l, 0, j)),
        name="adaln_mod",
    )(c, ada_w, ada_b.reshape(depth, 1, n_mod))


def _bd(xq, lane_lo):
    z = jnp.zeros((CHUNK, LANES), xq.dtype)
    zero = jnp.zeros((), xq.dtype)
    x0 = xq[:, :LANES]
    x1 = xq[:, LANES:]
    rows = [
        jnp.concatenate([jnp.where(lane_lo, x0, zero), z], axis=1),
        jnp.concatenate([jnp.where(lane_lo, zero, x0), z], axis=1),
        jnp.concatenate([z, jnp.where(lane_lo, x1, zero)], axis=1),
        jnp.concatenate([z, jnp.where(lane_lo, zero, x1)], axis=1),
    ]
    return jnp.concatenate(rows, axis=0)


def _mm_bd(a, xq, lane_lo):
    return _dot(a.astype(BF16), _bd(xq.astype(BF16), lane_lo))


def _head_transpose(xq):
    halves = []
    for i in range(QUAD // LANES):
        t = xq[:, i * LANES:(i + 1) * LANES].T
        halves.append(jnp.concatenate([t[:HEAD], t[HEAD:]], axis=1))
    return jnp.concatenate(halves, axis=1)


def _wkv_stages(ops_ref, y_ref, h_ref):
    row = lax.broadcasted_iota(jnp.int32, (CHUNK, QUAD), 0)
    col = lax.broadcasted_iota(jnp.int32, (CHUNK, QUAD), 1) % HEAD
    strict = col < row
    incl = col <= row
    eye_q = (col == row).astype(F32)
    lane_lo = lax.broadcasted_iota(jnp.int32, (CHUNK, LANES), 1) < HEAD
    tri2 = (lax.broadcasted_iota(jnp.int32, (CHUNK, 2 * CHUNK), 1) % CHUNK
            <= lax.broadcasted_iota(jnp.int32, (CHUNK, 2 * CHUNK), 0)).astype(BF16)

    n_chunks = ops_ref.shape[1] // CHUNK
    n_quads = ops_ref.shape[2] // QUAD
    state = [h_ref[q] for q in range(n_quads)]

    def chunk_inputs(ci):
        rows = slice(ci * CHUNK, (ci + 1) * CHUNK)
        lw = ops_ref[1, rows, :]
        cum = _dot(tri2, jnp.concatenate(_split(lw), axis=0))
        cum_end = cum[CHUNK - 1:CHUNK, :]
        e_out = jnp.exp(-cum)
        e_tail = jnp.exp(cum_end - cum)
        k_raw = ops_ref[2, rows, :]
        b_raw = ops_ref[5, rows, :]
        return dict(p_end=jnp.exp(cum_end),
                    r=ops_ref[0, rows, :] * jnp.exp(cum),
                    kk=ops_ref[4, rows, :] * jnp.exp(cum - lw),
                    k=k_raw * e_out, b=b_raw * e_out, k_hat=k_raw * e_tail, b_hat=b_raw * e_tail,
                    v=ops_ref[3, rows, :])

    def chain(ci, q, inp):
        sl = slice(q * QUAD, (q + 1) * QUAD)
        rq, kkq, vq = inp["r"][:, sl], inp["kk"][:, sl], inp["v"][:, sl]
        lhs = jnp.concatenate([kkq, rq], axis=0)
        a_k = _mm_bd(lhs, _head_transpose(inp["k"][:, sl]), lane_lo)
        a_b = _mm_bd(lhs, _head_transpose(inp["b"][:, sl]), lane_lo)
        a_kk = jnp.where(strict, a_k[:CHUNK], 0.0)
        a_rk = jnp.where(incl, a_k[CHUNK:], 0.0)
        a_kb = jnp.where(strict, a_b[:CHUNK], 0.0)
        a_rb = jnp.where(incl, a_b[CHUNK:], 0.0)
        bt = _head_transpose(inp["b_hat"][:, sl])
        kt = _head_transpose(inp["k_hat"][:, sl])
        yield
        t_inv = eye_q - a_kb
        pw = _mm_bd(a_kb, a_kb, lane_lo)
        av = _mm_bd(jnp.concatenate([a_kk, a_rk, kt], axis=0), vq, lane_lo)
        akk_v, ark_v, kt_v = av[:CHUNK], av[CHUNK:2 * CHUNK], av[2 * CHUNK:]
        yield
        for _ in range(4):
            both = _mm_bd(jnp.concatenate([pw, t_inv], axis=0), pw, lane_lo)
            pw = both[:CHUNK]
            t_inv = t_inv + both[CHUNK:]
            yield
        t_inv = t_inv + _mm_bd(t_inv, pw, lane_lo)
        yield
        kk_hat = _mm_bd(t_inv, kkq, lane_lo)
        u0 = _mm_bd(t_inv, akk_v, lane_lo)
        yield
        lhs2 = jnp.concatenate([a_rb, bt], axis=0)
        p1 = _mm_bd(lhs2, kk_hat, lane_lo)
        p2 = _mm_bd(lhs2, u0, lane_lo)
        r_hat = rq - p1[:CHUNK]
        y0 = ark_v - p2[:CHUNK]
        m_q = eye_q * inp["p_end"][:, sl] - p1[CHUNK:]
        g_q = kt_v - p2[CHUNK:]
        yield
        m_hi, m_lo = _split(m_q)
        h_hi, h_lo = _split(state[q])
        out = _dot(jnp.concatenate([r_hat.astype(BF16), m_hi, m_lo], axis=0), _bd(h_hi, lane_lo))
        y_ref[0, ci * CHUNK:(ci + 1) * CHUNK, sl] = out[:CHUNK] + y0
        state[q] = out[CHUNK:2 * CHUNK] + (out[2 * CHUNK:] + _dot(m_hi, _bd(h_lo, lane_lo))) + g_q
        yield

    gens = [[chain(ci, q, inp) for q in range(n_quads)] for ci, inp in
            ((ci, chunk_inputs(ci)) for ci in range(n_chunks))]
    yield 600
    n_stages = 10
    for tick in range(n_stages + n_chunks - 1):
        active = 0
        for ci in range(n_chunks):
            if 0 <= tick - ci < n_stages:
                active += 1
                for gen in gens[ci]:
                    next(gen)
        yield 110 * active
    for q in range(n_quads):
        h_ref[q] = state[q]


def _front_stages(has_vres, in_refs, out_refs, scr_refs):
    x_ref, mod_ref = in_refs[:2]
    (g1_ref, w_ref, wgv_ref, mu_ref, vec_ref, up1_ref,
     up2_ref) = [r.at[0] for r in in_refs[2:9]]
    ones_ref = in_refs[9]
    vfirst_ref = in_refs[10] if has_vres else None
    if has_vres:
        y_ref, g_ref, bonus_ref = out_refs
        vout_ref = None
    else:
        y_ref, g_ref, bonus_ref, vout_ref = out_refs
    sh_ref, ops_ref, h_ref = scr_refs
    s = pl.program_id(1)
    ts = x_ref.shape[1]
    d = x_ref.shape[2]

    @pl.when(s == 0)
    def _():
        sh_ref[0:SHIFT_PAD, :] = jnp.zeros((SHIFT_PAD, N_SHIFTED), F32)
        h_ref[...] = jnp.zeros_like(h_ref)

    @pl.when(s > 0)
    def _():
        sh_ref[0:SHIFT_PAD, :] = sh_ref[ts:ts + SHIFT_PAD, :]

    mod = mod_ref[0]
    h = _rms_mod(x_ref[0], g1_ref[...], mod[:, d:2 * d], mod[:, 0:d]).astype(BF16)
    yield 300
    for c0 in range(0, C_GV, 512):
        c1 = min(c0 + 512, C_GV)
        sh_ref[SHIFT_PAD:SHIFT_PAD + ts, c0:c1] = _dot(h, w_ref[:, c0:c1])
        yield 600
    sh_ref[SHIFT_PAD:SHIFT_PAD + ts, C_GV:N_SHIFTED] = _dot(h, wgv_ref[...])
    yield 300

    def shifted(c0, c1):
        both = sh_ref[:, c0:c1]
        cur = both[SHIFT_PAD:, :]
        prev = pltpu.roll(both, 1, 0)[SHIFT_PAD:, :]
        return cur + (prev - cur) * mu_ref[:, c0:c1]

    w0, a0, v0 = vec_ref[0:1, :], vec_ref[1:2, :], vec_ref[2:3, :]
    k_k, k_a, r_k = vec_ref[3:4, :], vec_ref[4:5, :], vec_ref[5:6, :]

    wa = shifted(C_WA, C_GV)
    lane = lax.broadcasted_iota(jnp.int32, wa.shape, 1)
    wa = jnp.where(lane < 64, jnp.tanh(wa), wa)
    lora1 = _dot(wa.astype(BF16), up1_ref[...])
    gv = shifted(C_GV, N_SHIFTED)
    lane = lax.broadcasted_iota(jnp.int32, gv.shape, 1)
    gv = jnp.where(lane < N_GATE_LORA, _sigmoid(gv), gv)
    lora2 = _dot(gv.astype(BF16), up2_ref[...])
    yield 400

    xw = w0 + lora1[:, :D_RWKV]
    softplus = jnp.maximum(-xw, 0.0) + jnp.log(1.0 + jnp.exp(-jnp.abs(xw)))
    ops_ref[1] = -jnp.exp(-softplus - 0.5)
    a = _sigmoid(a0 + lora1[:, D_RWKV:])
    g_ref[0] = lora2[:, :D_RWKV]
    yield 500

    r = shifted(C_R, C_K)
    v = shifted(C_V, C_WA)
    if has_vres:
        v = v + (vfirst_ref[0] - v) * _sigmoid(v0 + lora2[:, D_RWKV:])
    else:
        vout_ref[0] = v
    ops_ref[0] = r
    ops_ref[3] = v
    yield 500

    k = shifted(C_K, C_V)
    kk = k * k_k
    sq_hi, sq_lo = _split(kk * kk)
    n2 = _diag_dot(sq_hi, ones_ref) + _diag_dot(sq_lo, ones_ref)
    yield 400
    kk = kk / jnp.maximum(jnp.sqrt(n2), 1e-12)
    ops_ref[4] = kk
    ops_ref[5] = a * kk
    k = k * (1.0 + (a - 1.0) * k_a)
    ops_ref[2] = k
    bonus_ref[0] = _diag_dot((r * k * r_k).astype(BF16), ones_ref) * v
    yield 600

    yield from _wkv_stages(ops_ref, y_ref, h_ref)


def _back_stages(final, merge, in_refs, out_refs, scr_refs):
    y_ref, g_ref, bonus_ref, x_ref, mod_ref = in_refs[:5]
    (vec_ref, g1_ref, g2_ref, wgp_ref, poolw_ref, pa_ref, pb_ref, wout_ref, wgu_ref,
     wd_ref) = [r.at[0] for r in in_refs[5:15]]
    avg_ref = in_refs[15]
    gf_ref = in_refs[16] if final else None
    (o_ref,) = out_refs
    if merge:
        other_ref = in_refs[-1]
        o_ref[0] = other_ref[...]
        o_ref = o_ref.at[1]
    halo_ref, act_ref = scr_refs
    s = pl.program_id(1)
    ts = x_ref.shape[1]
    d = x_ref.shape[2]
    n_pool = halo_ref.shape[1]
    lnx_g, lnx_b, pool_scale = vec_ref[0:1, :], vec_ref[1:2, :], vec_ref[2:3, :]
    mod = mod_ref[0]

    @pl.when(s == 0)
    def _():
        halo_ref[0:POOL_HALO, :] = jnp.zeros((POOL_HALO, halo_ref.shape[1]), F32)

    @pl.when(s > 0)
    def _():
        halo_ref[0:POOL_HALO, :] = halo_ref[ts:ts + POOL_HALO, :]

    y = y_ref[0]
    mean = _diag_dot(y.astype(BF16), avg_ref)
    yield 100
    x = x_ref[0]
    h1 = _rms_mod(x, g1_ref[...], mod[:, d:2 * d], mod[:, 0:d]).astype(BF16)
    yield 300
    z = _dot(h1, wgp_ref[:, 0:n_pool])
    dev = y - mean
    yield 300
    gate_a = _dot(h1, wgp_ref[:, n_pool:n_pool + d])
    yield 550
    var = _diag_dot((dev * dev).astype(BF16), avg_ref)
    yield 100
    gate_b = _dot(h1, wgp_ref[:, n_pool + d:n_pool + 2 * d])
    halo_ref[POOL_HALO:POOL_HALO + ts, :] = z
    pos = (s * ts + 1 + lax.broadcasted_iota(jnp.int32, (ts, LANES), 0)).astype(F32)
    pooled = []
    for gi, win in enumerate(POOL_WINDOWS):
        acc = halo_ref[:, gi * LANES:(gi + 1) * LANES]
        span = 1
        while span < win:
            acc = acc + pltpu.roll(acc, span, 0)
            span *= 2
        mean_g = acc[POOL_HALO:, :] / jnp.minimum(pos, float(win))
        pooled.append(mean_g - z[:, gi * LANES:(gi + 1) * LANES])
    pooled = jnp.concatenate(pooled, axis=1)
    yield 550
    yb = _diag_dot(pooled.astype(BF16), poolw_ref) * pool_scale
    ya = (dev * lax.rsqrt(var + EPS_GN) * lnx_g + lnx_b + bonus_ref[0]) * g_ref[0]
    yield 200
    proj_a = _dot(ya.astype(BF16), pa_ref[...])
    yield 260
    proj_b = _dot(yb.astype(BF16), pb_ref[...])
    mixed = _sigmoid(gate_a) * proj_a + _sigmoid(gate_b) * proj_b
    yield 300
    x = x + mod[:, 2 * d:3 * d] * _dot(mixed.astype(BF16), wout_ref[...])
    yield 550

    d_ff = wd_ref.shape[0]
    h = _rms_mod(x, g2_ref[...], mod[:, 4 * d:5 * d], mod[:, 3 * d:4 * d]).astype(BF16)
    yield 300
    for j in range(0, d_ff, FF_BLOCK):
        w = min(FF_BLOCK, d_ff - j)
        gate = _dot(h, wgu_ref[:, j:j + w])
        up = _dot(h, wgu_ref[:, d_ff + j:d_ff + j + w])
        act_ref[:, j:j + w] = (gate * _sigmoid(gate) * up).astype(BF16)
        yield 2 * w
    act = act_ref[...]
    gt2 = mod[:, 5 * d:6 * d]
    outs = []
    for j in range(0, d, DOWN_BLOCK):
        outs.append(x[:, j:j + DOWN_BLOCK] + gt2[:, j:j + DOWN_BLOCK] * _dot(act, wd_ref[:, j:j + DOWN_BLOCK]))
        yield 2.8 * DOWN_BLOCK
    out = jnp.concatenate(outs, axis=1)
    if final:
        ms = jnp.mean(out * out, axis=-1, keepdims=True)
        out = out * lax.rsqrt(ms + EPS_RMS) * gf_ref[...]
    o_ref[0] = out
    yield 200


class _Part(NamedTuple):
    name: str
    inputs: Sequence[Any]
    in_specs: Sequence[Any]
    out_shapes: Sequence[Any]
    out_specs: Sequence[Any]
    scratch: Sequence[Any]
    stages: Callable[..., Any]
    cost: float


def _tok_spec(ts, n, row0):
    return pl.BlockSpec((1, ts, n), lambda i, s: (i + row0, s, 0))


def _row_spec(n, row0):
    return pl.BlockSpec((1, 1, n), lambda i, s: (i + row0, 0, 0))


def _const_spec(a):
    return pl.BlockSpec(a.shape, lambda i, s: (0,) * a.ndim, pipeline_mode=pl.Buffered(1))


def _layer_spec(a, layer, cols=None):
    shape = (1,) + a.shape[1:-1] + (cols or a.shape[-1],)
    return pl.BlockSpec(shape, lambda i, s: (layer,) + (0,) * (a.ndim - 1), pipeline_mode=pl.Buffered(1))


def _front_part(ts, x, x_row0, mod, mod_row0, rows, layer, stacked, ones_bd, vfirst):
    seq, d = x.shape[1], x.shape[2]
    has_vres = vfirst is not None
    inputs = [x, mod] + list(stacked) + [ones_bd]
    in_specs = ([_tok_spec(ts, d, x_row0), _row_spec(mod.shape[2], mod_row0)]
                + [_layer_spec(a, layer, C_GV if j == 1 else None) for j, a in enumerate(stacked)]
                + [_const_spec(ones_bd)])
    if has_vres:
        inputs.append(vfirst)
        in_specs.append(_tok_spec(ts, D_RWKV, 0))
    n_out = 3 if has_vres else 4
    return _Part(
        name="front",
        inputs=inputs, in_specs=in_specs,
        out_shapes=[jax.ShapeDtypeStruct((rows, seq, D_RWKV), F32)] * n_out,
        out_specs=[_tok_spec(ts, D_RWKV, 0)] * n_out,
        scratch=[pltpu.VMEM((ts + SHIFT_PAD, N_SHIFTED), F32), pltpu.VMEM((6, ts, D_RWKV), F32),
                 pltpu.VMEM((D_RWKV // QUAD, HEAD, QUAD), F32)],
        stages=functools.partial(_front_stages, has_vres),
        cost=10100.0)


def _back_part(ts, y, g, bonus, x, x_row0, mod, mod_row0, rows, layer, stacked, avg_bd, final_g, merge_with=None):
    seq, d = x.shape[1], x.shape[2]
    final = final_g is not None
    merge = merge_with is not None
    consts = [avg_bd] + ([final_g] if final else [])
    inputs = [y, g, bonus, x, mod] + list(stacked) + consts
    in_specs = ([_tok_spec(ts, D_RWKV, 0)] * 3 + [_tok_spec(ts, d, x_row0), _row_spec(mod.shape[2], mod_row0)]
                + [_layer_spec(a, layer) for a in stacked] + [_const_spec(a) for a in consts])
    out_shape = jax.ShapeDtypeStruct((rows, seq, d), F32)
    out_spec = _tok_spec(ts, d, 0)
    if merge:
        inputs.append(merge_with)
        in_specs.append(_tok_spec(ts, d, 0))
        out_shape = jax.ShapeDtypeStruct((2, rows, seq, d), F32)
        out_spec = pl.BlockSpec((2, 1, ts, d), lambda i, s: (0, i, s, 0))
    pool_bd, w_down = stacked[4], stacked[9]
    return _Part(
        name=("back_final" if final else "back") + ("_merge" if merge else ""),
        inputs=inputs, in_specs=in_specs,
        out_shapes=[out_shape],
        out_specs=[out_spec],
        scratch=[pltpu.VMEM((ts + POOL_HALO, pool_bd.shape[1]), F32), pltpu.VMEM((ts, w_down.shape[1]), BF16)],
        stages=functools.partial(_back_stages, final, merge),
        cost=12300.0)


def _cast_stages(in_refs, out_refs, scr_refs):
    for a_ref, o_ref in zip(in_refs, out_refs):
        o_ref[...] = a_ref[...].astype(o_ref.dtype)
        yield 1.0


def _cast_part(arrays, grid_shape):
    n_steps = grid_shape[0] * grid_shape[1]
    flat = [a.reshape(-1, a.shape[-1]) for a in arrays]
    for a in flat:
        assert a.shape[0] % (16 * n_steps) == 0, a.shape
    specs = [pl.BlockSpec((a.shape[0] // n_steps, a.shape[1]), lambda i, s: (i * grid_shape[1] + s, 0)) for a in flat]
    return _Part(
        name="cast",
        inputs=flat, in_specs=specs,
        out_shapes=[jax.ShapeDtypeStruct(a.shape, BF16) for a in flat],
        out_specs=specs,
        scratch=[],
        stages=_cast_stages,
        cost=float(len(flat)))


def _parts_kernel(parts, *refs):
    n_in = [len(p.inputs) for p in parts]
    n_out = [len(p.out_shapes) for p in parts]
    n_scr = [len(p.scratch) for p in parts]
    pos = 0
    ins, outs, scrs = [], [], []
    for n in n_in:
        ins.append(refs[pos:pos + n])
        pos += n
    for n in n_out:
        outs.append(refs[pos:pos + n])
        pos += n
    for n in n_scr:
        scrs.append(refs[pos:pos + n])
        pos += n
    gens = [p.stages(ins[i], outs[i], scrs[i]) for i, p in enumerate(parts)]
    done = [0.0] * len(parts)
    live = set(range(len(parts)))
    while live:
        i = min(live, key=lambda j: done[j] / parts[j].cost)
        try:
            done[i] += next(gens[i])
        except StopIteration:
            live.discard(i)


def _run_parts(parts, rows, tiles):
    outs = pl.pallas_call(
        functools.partial(_parts_kernel, parts),
        out_shape=[s for p in parts for s in p.out_shapes],
        grid=(rows, tiles),
        in_specs=[s for p in parts for s in p.in_specs],
        out_specs=[s for p in parts for s in p.out_specs],
        scratch_shapes=[s for p in parts for s in p.scratch],
        compiler_params=pltpu.CompilerParams(
            dimension_semantics=("arbitrary", "arbitrary"), vmem_limit_bytes=VMEM_LIMIT),
        name="_".join(p.name for p in parts),
    )(*[a for p in parts for a in p.inputs])
    split, pos = [], 0
    for p in parts:
        split.append(outs[pos:pos + len(p.out_shapes)])
        pos += len(p.out_shapes)
    return split


def _block_diag(blocks):
    n = len(blocks)
    rows = []
    for i, blk in enumerate(blocks):
        rows.append(jnp.concatenate(
            [blk if j == i else jnp.zeros((blk.shape[0], blocks[j].shape[1]), blk.dtype) for j in range(n)], axis=1))
    return jnp.concatenate(rows, axis=0)


def _first_layer_zero(p):
    return jnp.concatenate([jnp.zeros((1,) + p.shape[1:], p.dtype), p], axis=0)


def _block_diag_stacked(blocks):
    depth = blocks[0].shape[0]
    rows = []
    for i, blk in enumerate(blocks):
        rows.append(jnp.concatenate(
            [blk if j == i else jnp.zeros((depth, blk.shape[1], other.shape[2]), blk.dtype)
             for j, other in enumerate(blocks)], axis=2))
    return jnp.concatenate(rows, axis=1)


def kernel(x, c, ada_w, ada_b, norm1_g, w_in, v_down, mu_shift, mu_v, w0, w_up, a0, a_up, v0, v_up, g_up, k_k, k_a, r_k, lnx_g, lnx_b, pool_w, pool_scale, proj_a, proj_b, w_out, norm2_g, w_gu, w_down, final_g):
    depth = w_in.shape[0]
    batch, seq, d = x.shape
    half = batch // 2
    mods = _modulation(c, ada_w, ada_b).reshape(depth, batch, 1, 6 * d)
    n_heads = D_RWKV // HEAD
    ones_bd = _block_diag([jnp.ones((HEAD, HEAD), BF16)] * n_heads)
    avg_bd = _block_diag([jnp.full((HEAD, HEAD), 1.0 / HEAD, BF16)] * n_heads)

    n_shift = mu_shift.shape[1]
    n_vl = v_up.shape[1]
    n_pad = N_SHIFTED - n_shift - n_vl
    w_in16 = w_in.astype(BF16)
    w_gv = jnp.concatenate([w_in[:, :, C_GV:n_shift], _first_layer_zero(v_down), jnp.zeros((depth, d, n_pad), F32)],
                           axis=2).astype(BF16)
    w_gp = w_in16[:, :, n_shift:]
    mu = jnp.concatenate([mu_shift, _first_layer_zero(mu_v), jnp.zeros((depth, n_pad), F32)],
                         axis=1).reshape(depth, 1, N_SHIFTED)
    zeros_r = jnp.zeros((depth, D_RWKV), F32)
    vecs = jnp.stack([w0, a0, _first_layer_zero(v0), k_k, k_a, r_k.reshape(depth, D_RWKV), zeros_r, zeros_r], axis=1)
    up1 = _block_diag_stacked([w_up, a_up]).astype(BF16)
    up2 = jnp.concatenate([_block_diag_stacked([g_up, _first_layer_zero(v_up)]),
                           jnp.zeros((depth, n_pad, 2 * D_RWKV), F32)], axis=1).astype(BF16)
    g1 = norm1_g.reshape(depth, 1, d)
    front_w = (g1, w_in16, w_gv, mu, vecs, up1, up2)
    mix_vecs = jnp.stack([lnx_g, lnx_b, pool_scale] + [zeros_r] * 5, axis=1)
    pool_bd = _block_diag_stacked([pool_w[:, i] for i in range(pool_w.shape[1])]).astype(BF16)
    final_g2 = final_g.reshape(1, d)

    xs = [(x, 0), (x, half)]
    v_first = [None, None]
    fronts = {}

    def front(l, h, ts=TOKEN_TILE):
        xa, r0 = xs[h]
        return _front_part(ts, xa, r0, mods[l], h * half, half, l, front_w, ones_bd, v_first[h])

    def back(l, h, ts=TOKEN_TILE, merge_with=None):
        y, g, bonus = fronts[(l, h)]
        xa, r0 = xs[h]
        return _back_part(ts, y, g, bonus, xa, r0, mods[l], h * half, half, l, back_w, avg_bd,
                          final_g2 if l == depth - 1 else None, merge_with)

    def finish_front(l, h, outs):
        if l == 0:
            v_first[h] = outs[3]
        fronts[(l, h)] = outs[:3]

    tiles = seq // TOKEN_TILE
    solo_ts = SOLO_TOKEN_TILE if seq % SOLO_TOKEN_TILE == 0 else TOKEN_TILE
    back_f32 = (proj_a, proj_b, w_out, w_gu, w_down)
    o, cast = _run_parts([front(0, 0, solo_ts), _cast_part(back_f32, (half, seq // solo_ts))], half, seq // solo_ts)
    finish_front(0, 0, o)
    back_w = (mix_vecs, g1, norm2_g.reshape(depth, 1, d), w_gp, pool_bd) + tuple(
        c.reshape(a.shape) for c, a in zip(cast, back_f32))
    for l in range(depth):
        o_f, o_b = _run_parts([front(l, 1), back(l, 0)], half, tiles)
        finish_front(l, 1, o_f)
        xs[0] = (o_b[0], 0)
        if l + 1 < depth:
            o_f, o_b = _run_parts([front(l + 1, 0), back(l, 1)], half, tiles)
            finish_front(l + 1, 0, o_f)
            xs[1] = (o_b[0], 0)
    (o_b,) = _run_parts([back(depth - 1, 1, solo_ts, merge_with=xs[0][0])], half, seq // solo_ts)
    return o_b[0].reshape(batch, seq, d)
```
